```python
import math
import jax, jax.numpy as jnp
from jax import lax
import numpy as np

D_MODEL = 1024
BATCH = 8
SEQ = 2048
DEPTH = 1
DEC_BATCH = 8
DEC_SEQ = 64
PAST_LEN = 4096

CHUNK = 64
GMLP_BLOCK = 128
D_V = 1024
GMLP_HEADS = 4
GMLP_HEAD_DIM = D_V // GMLP_HEADS
D_POOL = 1024
POOL_WINDOWS = (2, 4, 8, 16)
POOL_GROUPS = len(POOL_WINDOWS)
POOL_GROUP_DIM = D_POOL // POOL_GROUPS
POOL_STATE = max(POOL_WINDOWS) - 1
D_IN = 2 * D_V + D_POOL + 2 * D_MODEL
N_EXPERTS = 32
TOP_K = 4
D_FF = 1024
SWIGLU_LIMIT = 7.0
SWIGLU_ALPHA = 1.702
PLE_DIM = 256
EPS = 1e-6

kernel_name = "gated_gmlp_pool_moe_streaming_step"


def _rmsnorm(x, g):
    xf = x.astype(jnp.float32)
    y = xf * lax.rsqrt(jnp.mean(xf * xf, axis=-1, keepdims=True) + EPS)
    return (y * g.astype(jnp.float32)).astype(x.dtype)


def _layernorm(x, g, b):
    xf = x.astype(jnp.float32)
    mu = jnp.mean(xf, axis=-1, keepdims=True)
    xc = xf - mu
    y = xc * lax.rsqrt(jnp.mean(xc * xc, axis=-1, keepdims=True) + EPS)
    return (y * g.astype(jnp.float32) + b.astype(jnp.float32)).astype(x.dtype)


def _spatial_gate(v_n, w_spatial, b_spatial):
    B, L, _ = v_n.shape
    n_blk = -(-L // GMLP_BLOCK)
    Lp = n_blk * GMLP_BLOCK
    vp = jnp.pad(v_n, ((0, 0), (0, Lp - L), (0, 0)))
    vr = vp.reshape(B, n_blk, GMLP_BLOCK, GMLP_HEADS, GMLP_HEAD_DIM)
    cidx = jnp.arange(GMLP_BLOCK) // CHUNK
    mask = (cidx[:, None] >= cidx[None, :]).astype(w_spatial.dtype)
    wm = w_spatial * mask[None]
    s = jnp.einsum("hij,bnjhd->bnihd", wm, vr) + jnp.transpose(b_spatial)[:, :, None]
    return s.reshape(B, Lp, D_V)[:, :L]


def _pool_mix(pb, prev, pos0, w_pool_group, pool_scale):
    B, L, _ = pb.shape
    P = POOL_STATE
    xcat = jnp.concatenate([prev, pb], axis=1)
    cs = jnp.cumsum(xcat.astype(jnp.float32), axis=1)
    cs0 = jnp.pad(cs, ((0, 0), (1, 0), (0, 0)))
    pos = pos0 + jnp.arange(L)
    means = []
    for g, w in enumerate(POOL_WINDOWS):
        c = cs0[:, :, g * POOL_GROUP_DIM:(g + 1) * POOL_GROUP_DIM]
        s = c[:, P + 1:P + 1 + L] - c[:, P + 1 - w:P + 1 - w + L]
        cnt = jnp.minimum(pos + 1, w).astype(jnp.float32)
        means.append(s / cnt[None, :, None])
    mean = jnp.concatenate(means, axis=-1).astype(pb.dtype)
    z = (mean - pb).reshape(B, L, POOL_GROUPS, POOL_GROUP_DIM)
    z = jnp.einsum("blgc,gcd->blgd", z, w_pool_group).reshape(B, L, D_POOL) * pool_scale
    return z, xcat[:, -P:]


def _moe(xn, w_router, b_router, w_up, b_up, w_down, b_down):
    B, L, D = xn.shape
    t = xn.reshape(B * L, D)
    logits = (t @ w_router + b_router).astype(jnp.float32)
    top_v, top_i = lax.top_k(logits, TOP_K)
    top_w = jax.nn.softmax(top_v, axis=-1)
    gate = jnp.sum(jax.nn.one_hot(top_i, N_EXPERTS, dtype=jnp.float32) * top_w[..., None], axis=1)
    out = jnp.zeros((B * L, D), jnp.float32)
    for e in range(N_EXPERTS):
        hid = t @ w_up[e] + b_up[e]
        x_glu = jnp.minimum(hid[:, :D_FF], SWIGLU_LIMIT)
        x_lin = jnp.clip(hid[:, D_FF:], -SWIGLU_LIMIT, SWIGLU_LIMIT)
        act = x_glu * jax.nn.sigmoid(SWIGLU_ALPHA * x_glu) * (x_lin + 1.0)
        out = out + gate[:, e:e + 1] * (act @ w_down[e] + b_down[e]).astype(jnp.float32)
    return out.astype(xn.dtype).reshape(B, L, D)


def _layer(h, p, pool_prev, pos0, norm_mix, w_in, ln_v_g, ln_v_b, w_spatial, b_spatial,
           w_pool_group, pool_scale, w_branch_a, w_branch_b, w_out, norm_moe, w_router,
           b_router, w_up, b_up, w_down, b_down, norm_ple, w_ple_gate, w_ple_proj):
    n = _rmsnorm(h, norm_mix)
    proj = n @ w_in
    o1, o2, o3, o4 = D_V, 2 * D_V, 2 * D_V + D_POOL, 2 * D_V + D_POOL + D_MODEL
    u = jax.nn.gelu(proj[..., :o1])
    v = jax.nn.gelu(proj[..., o1:o2])
    pb = proj[..., o2:o3]
    g_a = jax.nn.sigmoid(proj[..., o3:o4])
    g_b = jax.nn.sigmoid(proj[..., o4:])
    v_n = _layernorm(v, ln_v_g, ln_v_b)
    y_a = (u * _spatial_gate(v_n, w_spatial, b_spatial)) @ w_branch_a
    z, pool_next = _pool_mix(pb, pool_prev, pos0, w_pool_group, pool_scale)
    y_b = z @ w_branch_b
    h = h + (g_a * y_a + g_b * y_b) @ w_out
    h = h + _moe(_rmsnorm(h, norm_moe), w_router, b_router, w_up, b_up, w_down, b_down)
    n3 = _rmsnorm(h, norm_ple)
    h = h + jax.nn.sigmoid(n3 @ w_ple_gate) * (p @ w_ple_proj)
    return h, pool_next, v_n


def setup_inputs(seed: int = 0) -> dict:
    key = jax.random.key(seed)
    ks = jax.random.split(key, 32)
    f32 = jnp.float32
    nrm = lambda k, s, sc: jax.random.normal(k, s, f32) * sc
    return {
        "x_prompt": nrm(ks[0], (BATCH, SEQ, D_MODEL), 1.0),
        "x_sample": nrm(ks[1], (DEC_BATCH, DEC_SEQ, D_MODEL), 1.0),
        "cache_pool": nrm(ks[2], (DEPTH, DEC_BATCH, min(POOL_STATE, PAST_LEN), D_POOL), 1.0),
        "p_prompt": nrm(ks[3], (DEPTH, BATCH, SEQ, PLE_DIM), 1.0),
        "p_sample": nrm(ks[4], (DEPTH, DEC_BATCH, DEC_SEQ, PLE_DIM), 1.0),
        "norm_mix": 1.0 + nrm(ks[5], (DEPTH, D_MODEL), 0.05),
        "w_in": nrm(ks[6], (DEPTH, D_MODEL, D_IN), D_MODEL ** -0.5),
        "ln_v_g": 1.0 + nrm(ks[7], (DEPTH, D_V), 0.05),
        "ln_v_b": nrm(ks[8], (DEPTH, D_V), 0.02),
        "w_spatial": nrm(ks[9], (DEPTH, GMLP_HEADS, GMLP_BLOCK, GMLP_BLOCK), GMLP_BLOCK ** -0.5),
        "b_spatial": 1.0 + nrm(ks[10], (DEPTH, GMLP_HEADS, GMLP_BLOCK), 0.1),
        "w_pool_group": nrm(ks[11], (DEPTH, POOL_GROUPS, POOL_GROUP_DIM, POOL_GROUP_DIM), POOL_GROUP_DIM ** -0.5),
        "pool_scale": 1.0 + nrm(ks[12], (DEPTH, D_POOL), 0.1),
        "w_branch_a": nrm(ks[13], (DEPTH, D_V, D_MODEL), D_V ** -0.5),
        "w_branch_b": nrm(ks[14], (DEPTH, D_POOL, D_MODEL), D_POOL ** -0.5),
        "w_out": nrm(ks[15], (DEPTH, D_MODEL, D_MODEL), D_MODEL ** -0.5),
        "norm_moe": 1.0 + nrm(ks[16], (DEPTH, D_MODEL), 0.05),
        "w_router": nrm(ks[17], (DEPTH, D_MODEL, N_EXPERTS), D_MODEL ** -0.5),
        "b_router": nrm(ks[18], (DEPTH, N_EXPERTS), 0.01),
        "w_up": nrm(ks[19], (DEPTH, N_EXPERTS, D_MODEL, 2 * D_FF), D_MODEL ** -0.5),
        "b_up": nrm(ks[20], (DEPTH, N_EXPERTS, 2 * D_FF), 0.01),
        "w_down": nrm(ks[21], (DEPTH, N_EXPERTS, D_FF, D_MODEL), D_FF ** -0.5),
        "b_down": nrm(ks[22], (DEPTH, N_EXPERTS, D_MODEL), 0.01),
        "norm_ple": 1.0 + nrm(ks[23], (DEPTH, D_MODEL), 0.05),
        "w_ple_gate": nrm(ks[24], (DEPTH, D_MODEL, D_MODEL), D_MODEL ** -0.5),
        "w_ple_proj": nrm(ks[25], (DEPTH, PLE_DIM, D_MODEL), PLE_DIM ** -0.5),
        "norm_final": 1.0 + nrm(ks[26], (D_MODEL,), 0.05),
    }


def reference(x_prompt, x_sample, cache_pool, p_prompt, p_sample, norm_mix, w_in, ln_v_g,
              ln_v_b, w_spatial, b_spatial, w_pool_group, pool_scale, w_branch_a, w_branch_b,
              w_out, norm_moe, w_router, b_router, w_up, b_up, w_down, b_down, norm_ple,
              w_ple_gate, w_ple_proj, norm_final):
    h_p = x_prompt
    h_s = x_sample
    pool_zero = jnp.zeros((x_prompt.shape[0], POOL_STATE, D_POOL), x_prompt.dtype)
    pool_p_list, pool_s_list, v_s_list = [], [], []
    for i in range(DEPTH):
        w = (norm_mix[i], w_in[i], ln_v_g[i], ln_v_b[i], w_spatial[i], b_spatial[i],
             w_pool_group[i], pool_scale[i], w_branch_a[i], w_branch_b[i], w_out[i],
             norm_moe[i], w_router[i], b_router[i], w_up[i], b_up[i], w_down[i], b_down[i],
             norm_ple[i], w_ple_gate[i], w_ple_proj[i])
        h_p, pool_p, _ = _layer(h_p, p_prompt[i], pool_zero, 0, *w)
        h_s, pool_s, v_s = _layer(h_s, p_sample[i], cache_pool[i], PAST_LEN, *w)
        pool_p_list.append(pool_p)
        pool_s_list.append(pool_s)
        v_s_list.append(v_s)
    y_prompt = _rmsnorm(h_p, norm_final)
    y_sample = _rmsnorm(h_s, norm_final)
    new_pool_prompt = jnp.stack(pool_p_list, axis=0)
    new_pool_sample = jnp.stack(pool_s_list, axis=0)
    new_gmlp_v_sample = jnp.stack(v_s_list, axis=0)
    return (y_prompt, y_sample, new_pool_prompt, new_pool_sample, new_gmlp_v_sample)
```

```python
import functools

import jax
import jax.numpy as jnp
from jax import lax
from jax.experimental import pallas as pl
from jax.experimental.pallas import tpu as pltpu

F32 = jnp.float32
BF16 = jnp.bfloat16
I32 = jnp.int32
U32 = jnp.uint32

EPS = 1e-6
CHUNK = 64
GMLP_BLOCK = 128
GMLP_HEADS = 4
POOL_WINDOWS = (2, 4, 8, 16)
POOL_STATE = 15
HALO = 16
N_EXPERTS = 32
TOP_K = 4
SWIGLU_LIMIT = 7.0
SWIGLU_ALPHA = 1.702
LANES = 128

TOKEN_TILE = 256
ROW_TILE = 256
VMEM_LIMIT = 56 * 1024 * 1024


def _rms(xf, g):
    return xf * lax.rsqrt(jnp.mean(xf * xf, axis=-1, keepdims=True) + EPS) * g


def _dot(a, b):
    return jnp.dot(a, b, preferred_element_type=F32)


def _stage_a_kernel(n_seq, seq_rows, carry_halo, pos0, *refs):
    if carry_halo:
        (x_ref, nm_ref, win_ref, lng_ref, lnb_ref, wsp_ref, bsp_ref, wpool_ref, pscale_ref,
         wa_ref, wb_ref, wout_ref, nmoe_ref, wr_ref, br_ref,
         h1_ref, xn2_ref, lg_ref, pool_ref, vn_ref, xc_ref, a_ref) = refs
        cache_ref = None
    else:
        (x_ref, cache_ref, nm_ref, win_ref, lng_ref, lnb_ref, wsp_ref, bsp_ref, wpool_ref,
         pscale_ref, wa_ref, wb_ref, wout_ref, nmoe_ref, wr_ref, br_ref,
         h1_ref, xn2_ref, lg_ref, pool_ref, vn_ref, xc_ref, a_ref) = refs
    d = x_ref.shape[-1]
    head_dim = d // GMLP_HEADS
    group_dim = d // len(POOL_WINDOWS)
    blk = min(GMLP_BLOCK, seq_rows)
    j = pl.program_id(1) if carry_halo else 0

    x = x_ref[...]
    n = _rms(x, nm_ref[...]).astype(BF16)

    def proj(seg):
        return _dot(n, win_ref[:, seg * d:(seg + 1) * d])

    v = jax.nn.gelu(proj(1))
    mu = jnp.mean(v, axis=-1, keepdims=True)
    vc = v - mu
    vn_ref[...] = vc * lax.rsqrt(jnp.mean(vc * vc, axis=-1, keepdims=True) + EPS) * lng_ref[...] + lnb_ref[...]
    u = jax.nn.gelu(proj(0))

    ri = lax.broadcasted_iota(I32, (GMLP_BLOCK, GMLP_BLOCK), 0) // CHUNK
    ci = lax.broadcasted_iota(I32, (GMLP_BLOCK, GMLP_BLOCK), 1) // CHUNK
    causal = (ri >= ci).astype(F32)
    for h in range(GMLP_HEADS):
        wm = (wsp_ref[h] * causal)[:blk, :blk].astype(BF16)
        bias = bsp_ref[:blk, h:h + 1]
        cols = slice(h * head_dim, (h + 1) * head_dim)
        for r0 in range(0, n_seq * seq_rows, blk):
            rows = slice(r0, r0 + blk)
            sg = _dot(wm, vn_ref[rows, cols].astype(BF16)) + bias
            a_ref[rows, cols] = (u[rows, cols] * sg).astype(BF16)
    y_a = _dot(a_ref[...], wa_ref[...])

    pb = proj(2)
    if carry_halo:
        @pl.when(j == 0)
        def _():
            xc_ref[0, 0:HALO, :] = jnp.zeros((HALO, d), F32)
    else:
        for s in range(n_seq):
            xc_ref[s, 0:HALO, :] = jnp.zeros((HALO, d), F32)
            xc_ref[s, HALO - POOL_STATE:HALO, :] = cache_ref[s]
    for s in range(n_seq):
        xc_ref[s, HALO:HALO + seq_rows, :] = pb[s * seq_rows:(s + 1) * seq_rows]

    if carry_halo:
        pos = pos0 + j * seq_rows + lax.broadcasted_iota(I32, (seq_rows, 1), 0)
    z_groups = []
    for g, w in enumerate(POOL_WINDOWS):
        cols = slice(g * group_dim, (g + 1) * group_dim)
        if carry_halo:
            cnt = jnp.minimum(pos + 1, w).astype(F32)
        else:
            cnt = float(min(pos0 + 1, w))
        parts = []
        for s in range(n_seq):
            acc = xc_ref[s, HALO:HALO + seq_rows, cols]
            for k in range(1, w):
                acc = acc + xc_ref[s, HALO - k:HALO - k + seq_rows, cols]
            parts.append(acc / cnt - xc_ref[s, HALO:HALO + seq_rows, cols])
        zg = parts[0] if n_seq == 1 else jnp.concatenate(parts, axis=0)
        z_groups.append(_dot(zg.astype(BF16), wpool_ref[g]) * pscale_ref[:, cols])
    z = jnp.concatenate(z_groups, axis=1)
    y_b = _dot(z.astype(BF16), wb_ref[...])

    for s in range(n_seq):
        tail = xc_ref[s, HALO + seq_rows - POOL_STATE:HALO + seq_rows, :]
        if carry_halo:
            @pl.when(j == pl.num_programs(1) - 1)
            def _():
                pool_ref[...] = tail
        else:
            pool_ref[s] = tail
    if carry_halo:
        xc_ref[0, 0:HALO, :] = xc_ref[0, seq_rows:seq_rows + HALO, :]

    g_a = jax.nn.sigmoid(proj(3))
    g_b = jax.nn.sigmoid(proj(4))
    merged = (g_a * y_a + g_b * y_b).astype(BF16)
    h1 = x + _dot(merged, wout_ref[...])
    h1_ref[...] = h1

    xn2 = _rms(h1, nmoe_ref[...]).astype(BF16)
    lg_ref[...] = _dot(xn2, wr_ref[...]) + br_ref[...]
    xn2_ref[...] = xn2.astype(F32)


def _const_spec(shape, grid_rank):
    zeros = (0,) * len(shape)
    if grid_rank == 2:
        return pl.BlockSpec(shape, lambda b, j: zeros, pipeline_mode=pl.Buffered(1))
    return pl.BlockSpec(shape, lambda i: zeros, pipeline_mode=pl.Buffered(1))


def _stage_a_outputs(n_rows, d):
    return (jax.ShapeDtypeStruct((n_rows, d), F32),
            jax.ShapeDtypeStruct((n_rows, d), F32),
            jax.ShapeDtypeStruct((n_rows, LANES), F32))


def _stage_a_prompt(x_prompt, weights):
    nb, seq, d = x_prompt.shape
    tiles_per_seq = seq // TOKEN_TILE
    tok_map = lambda b, j: (b * tiles_per_seq + j, 0)
    return pl.pallas_call(
        functools.partial(_stage_a_kernel, 1, TOKEN_TILE, True, 0),
        grid=(nb, tiles_per_seq),
        in_specs=[pl.BlockSpec((None, TOKEN_TILE, d), lambda b, j: (b, j, 0))]
        + [_const_spec(w.shape, 2) for w in weights],
        out_specs=(
            pl.BlockSpec((TOKEN_TILE, d), tok_map),
            pl.BlockSpec((TOKEN_TILE, d), tok_map),
            pl.BlockSpec((TOKEN_TILE, LANES), tok_map),
            pl.BlockSpec((None, POOL_STATE, d), lambda b, j: (b, 0, 0)),
        ),
        out_shape=_stage_a_outputs(nb * seq, d) + (jax.ShapeDtypeStruct((nb, POOL_STATE, d), F32),),
        scratch_shapes=[
            pltpu.VMEM((TOKEN_TILE, d), F32),
            pltpu.VMEM((1, HALO + TOKEN_TILE, d), F32),
            pltpu.VMEM((TOKEN_TILE, d), BF16),
        ],
        compiler_params=pltpu.CompilerParams(
            dimension_semantics=("arbitrary", "arbitrary"), vmem_limit_bytes=VMEM_LIMIT),
        name="stage_a_prompt",
    )(x_prompt, *weights)


def _stage_a_sample(x_sample, cache, past_len, weights):
    sb, sseq, d = x_sample.shape
    n_s = sb * sseq
    return pl.pallas_call(
        functools.partial(_stage_a_kernel, sb, sseq, False, past_len),
        grid=(1,),
        in_specs=[pl.BlockSpec((n_s, d), lambda i: (0, 0)),
                  pl.BlockSpec((sb, POOL_STATE, d), lambda i: (0, 0, 0))]
        + [_const_spec(w.shape, 1) for w in weights],
        out_specs=(
            pl.BlockSpec((n_s, d), lambda i: (0, 0)),
            pl.BlockSpec((n_s, d), lambda i: (0, 0)),
            pl.BlockSpec((n_s, LANES), lambda i: (0, 0)),
            pl.BlockSpec((sb, POOL_STATE, d), lambda i: (0, 0, 0)),
            pl.BlockSpec((n_s, d), lambda i: (0, 0)),
        ),
        out_shape=_stage_a_outputs(n_s, d) + (jax.ShapeDtypeStruct((sb, POOL_STATE, d), F32),
                                              jax.ShapeDtypeStruct((n_s, d), F32)),
        scratch_shapes=[
            pltpu.VMEM((sb, HALO + sseq, d), F32),
            pltpu.VMEM((n_s, d), BF16),
        ],
        compiler_params=pltpu.CompilerParams(
            dimension_semantics=("arbitrary",), vmem_limit_bytes=VMEM_LIMIT),
        name="stage_a_sample",
    )(x_sample.reshape(n_s, d), cache, *weights)


def _route_kernel(lgp_ref, lgs_ref, pos_ref, w_ref, cnt_ref):
    rb = TOKEN_TILE
    n_p, n_s = lgp_ref.shape[0], lgs_ref.shape[0]
    lane = lax.broadcasted_iota(I32, (rb, LANES), 1).astype(F32)
    tri = (lax.broadcasted_iota(I32, (rb, rb), 0) > lax.broadcasted_iota(I32, (rb, rb), 1)).astype(BF16)
    neg = jnp.float32(-jnp.inf)

    def rank_block(lg_ref, row0, i, run):
        src = pl.ds(pl.multiple_of(i * rb, rb), rb)
        dst = pl.ds(pl.multiple_of(row0 + i * rb, rb), rb)
        lg = jnp.where(lane < N_EXPERTS, lg_ref[src, :], neg)
        picked = jnp.zeros((rb, LANES), F32)
        vals, idxs = [], []
        for _ in range(TOP_K):
            m = jnp.max(lg, axis=1, keepdims=True)
            idx = jnp.min(jnp.where(lg == m, lane, float(LANES)), axis=1, keepdims=True)
            sel = lane == idx
            lg = jnp.where(sel, neg, lg)
            picked = picked + sel.astype(F32)
            vals.append(m)
            idxs.append(idx)
        ex = [jnp.exp(v - vals[0]) for v in vals]
        den = ex[0] + ex[1] + ex[2] + ex[3]
        before = _dot(tri, picked.astype(BF16)) + run
        packed = jnp.zeros((rb, LANES), F32)
        wts = jnp.zeros((rb, LANES), F32)
        for k in range(TOP_K):
            rank = jnp.sum(jnp.where(lane == idxs[k], before, 0.0), axis=1, keepdims=True)
            packed = jnp.where(lane == k, rank, packed)
            packed = jnp.where(lane == TOP_K + k, idxs[k], packed)
            wts = jnp.where(lane == k, ex[k] / den, wts)
        pos_ref[dst, :] = packed.astype(I32)
        w_ref[dst, :] = wts
        return run + jnp.sum(picked, axis=0, keepdims=True)

    counts = lax.fori_loop(0, n_p // rb, functools.partial(rank_block, lgp_ref, 0), jnp.zeros((1, LANES), F32))
    counts = lax.fori_loop(0, n_s // rb, functools.partial(rank_block, lgs_ref, n_p), counts)
    cnt_ref[...] = jnp.broadcast_to(counts, cnt_ref.shape)

    tiles = jnp.floor((counts + (ROW_TILE - 1)) * (1.0 / ROW_TILE))
    upper = (lax.broadcasted_iota(I32, (LANES, LANES), 0) < lax.broadcasted_iota(I32, (LANES, LANES), 1)).astype(BF16)
    first = _dot(jnp.broadcast_to(tiles, (8, LANES)).astype(BF16), upper)[0:1, :] * float(ROW_TILE)

    def place_block(i, c):
        rows = pl.ds(pl.multiple_of(i * rb, rb), rb)
        packed = pos_ref[rows, :].astype(F32)
        out = packed
        for k in range(TOP_K):
            e_k = jnp.sum(jnp.where(lane == TOP_K + k, packed, 0.0), axis=1, keepdims=True)
            base = jnp.sum(jnp.where(lane == e_k, first, 0.0), axis=1, keepdims=True)
            out = jnp.where(lane == k, packed + base, out)
        pos_ref[rows, :] = out.astype(I32)
        return c

    lax.fori_loop(0, (n_p + n_s) // rb, place_block, 0)


def _route(lg_p, lg_s):
    n_tok = lg_p.shape[0] + lg_s.shape[0]
    return pl.pallas_call(
        _route_kernel,
        out_shape=(jax.ShapeDtypeStruct((n_tok, LANES), I32),
                   jax.ShapeDtypeStruct((n_tok, LANES), F32),
                   jax.ShapeDtypeStruct((8, LANES), F32)),
        compiler_params=pltpu.CompilerParams(vmem_limit_bytes=VMEM_LIMIT),
        name="route",
    )(lg_p, lg_s)


def _dispatch_kernel(n_prompt_tiles, pos_ref, xp_hbm, xsm_hbm, xs_in_hbm, xs_hbm, sem):
    del xs_in_hbm
    i = pl.program_id(0)

    def send_tile(src_hbm, base):
        def copy(r, k):
            return pltpu.make_async_copy(src_hbm.at[pl.ds(base + r, 1), :],
                                         xs_hbm.at[pl.ds(pos_ref[r * TOP_K + k], 1), :], sem)

        def start(r, c):
            for k in range(TOP_K):
                copy(r, k).start()
            return c

        def wait(r, c):
            for k in range(TOP_K):
                copy(r, k).wait()
            return c

        lax.fori_loop(0, TOKEN_TILE, start, 0)
        lax.fori_loop(0, TOKEN_TILE, wait, 0)

    @pl.when(i < n_prompt_tiles)
    def _():
        send_tile(xp_hbm, i * TOKEN_TILE)

    @pl.when(i >= n_prompt_tiles)
    def _():
        send_tile(xsm_hbm, (i - n_prompt_tiles) * TOKEN_TILE)


def _dispatch(pos_flat, xn2_p, xn2_s, n_rows):
    d = xn2_p.shape[1]
    npt = xn2_p.shape[0] // TOKEN_TILE
    nst = xn2_s.shape[0] // TOKEN_TILE
    any_spec = pl.BlockSpec(memory_space=pl.ANY)
    return pl.pallas_call(
        functools.partial(_dispatch_kernel, npt),
        grid=(npt + nst,),
        in_specs=[pl.BlockSpec((TOKEN_TILE * TOP_K,), lambda i: (i,), memory_space=pltpu.SMEM),
                  any_spec, any_spec, any_spec],
        out_specs=any_spec,
        out_shape=jax.ShapeDtypeStruct((n_rows, d), F32),
        scratch_shapes=[pltpu.SemaphoreType.DMA(())],
        input_output_aliases={3: 0},
        compiler_params=pltpu.CompilerParams(dimension_semantics=("arbitrary",)),
        name="dispatch",
    )(pos_flat, xn2_p, xn2_s, jnp.zeros((n_rows, d), F32))


def _expert_kernel(te_ref, nt_ref, xs_ref, wup_ref, bup_ref, wdn_ref, bdn_ref, ys_ref):
    d_ff = wdn_ref.shape[0]
    used = pl.program_id(0) < nt_ref[0]

    @pl.when(used)
    def _():
        hid = _dot(xs_ref[...].astype(BF16), wup_ref[...].astype(BF16)) + bup_ref[...]
        glu = jnp.minimum(hid[:, :d_ff], SWIGLU_LIMIT)
        lin = jnp.clip(hid[:, d_ff:], -SWIGLU_LIMIT, SWIGLU_LIMIT)
        act = glu * jax.nn.sigmoid(SWIGLU_ALPHA * glu) * (lin + 1.0)
        ys_ref[...] = _dot(act.astype(BF16), wdn_ref[...].astype(BF16)) + bdn_ref[...]

    @pl.when(jnp.logical_not(used))
    def _():
        ys_ref[...] = jnp.zeros(ys_ref.shape, F32)


def _experts(tile_expert, num_tiles, xs, w_up, b_up, w_down, b_down):
    n_rows = xs.shape[0]
    n_e, d, d_up = w_up.shape
    d_ff = w_down.shape[1]
    exp_map = lambda i, te, nt: (te[i], 0, 0)
    grid_spec = pltpu.PrefetchScalarGridSpec(
        num_scalar_prefetch=2,
        grid=(n_rows // ROW_TILE,),
        in_specs=[
            pl.BlockSpec((ROW_TILE, d), lambda i, te, nt: (jnp.minimum(i, nt[0] - 1), 0)),
            pl.BlockSpec((None, d, d_up), exp_map),
            pl.BlockSpec((None, 1, d_up), exp_map),
            pl.BlockSpec((None, d_ff, d), exp_map),
            pl.BlockSpec((None, 1, d), exp_map),
        ],
        out_specs=pl.BlockSpec((ROW_TILE, d), lambda i, te, nt: (i, 0)),
    )
    return pl.pallas_call(
        _expert_kernel,
        grid_spec=grid_spec,
        out_shape=jax.ShapeDtypeStruct((n_rows, d), F32),
        compiler_params=pltpu.CompilerParams(
            dimension_semantics=("arbitrary",), vmem_limit_bytes=VMEM_LIMIT),
        name="experts",
    )(tile_expert, num_tiles, xs, w_up, b_up.reshape(n_e, 1, d_up), w_down, b_down.reshape(n_e, 1, d))


def _combine_kernel(n_prompt_tiles, pos_ref, ys_hbm, w_ref, h1p_ref, h1s_ref, pp_ref, ps_ref,
                    nple_ref, wg_ref, wp_ref, nfin_ref, yp_ref, ysm_ref, gbuf, sem):
    i = pl.program_id(0)
    is_prompt = i < n_prompt_tiles

    def copy(r, k):
        return pltpu.make_async_copy(ys_hbm.at[pl.ds(pos_ref[r * TOP_K + k], 1), :],
                                     gbuf.at[k, pl.ds(r, 1), :], sem)

    def start(r, c):
        for k in range(TOP_K):
            copy(r, k).start()
        return c

    def wait(r, c):
        for k in range(TOP_K):
            copy(r, k).wait()
        return c

    lax.fori_loop(0, TOKEN_TILE, start, 0)
    lax.fori_loop(0, TOKEN_TILE, wait, 0)

    wts = w_ref[...]
    moe = wts[:, 0:1] * gbuf[0]
    for k in range(1, TOP_K):
        moe = moe + wts[:, k:k + 1] * gbuf[k]
    h2 = jnp.where(is_prompt, h1p_ref[...], h1s_ref[...]) + moe
    p = jnp.where(is_prompt, pp_ref[...], ps_ref[...])
    n3 = _rms(h2, nple_ref[...]).astype(BF16)
    gate = jax.nn.sigmoid(_dot(n3, wg_ref[...]))
    h3 = h2 + gate * _dot(p.astype(BF16), wp_ref[...])
    y = _rms(h3, nfin_ref[...])

    @pl.when(is_prompt)
    def _():
        yp_ref[...] = y

    @pl.when(jnp.logical_not(is_prompt))
    def _():
        ysm_ref[...] = y


def _combine(pos_flat, ys, topw, h1_p, h1_s, p_p, p_s, nple, wg, wp, nfin):
    n_p, d = h1_p.shape
    n_s = h1_s.shape[0]
    ple = p_p.shape[1]
    npt = n_p // TOKEN_TILE
    nst = n_s // TOKEN_TILE
    tok = lambda i: (i, 0)
    p_tok = lambda i: (jnp.minimum(i, npt - 1), 0)
    s_tok = lambda i: (jnp.maximum(i - npt, 0), 0)
    const = lambda i: (0, 0)
    return pl.pallas_call(
        functools.partial(_combine_kernel, npt),
        grid=(npt + nst,),
        in_specs=[
            pl.BlockSpec((TOKEN_TILE * TOP_K,), lambda i: (i,), memory_space=pltpu.SMEM),
            pl.BlockSpec(memory_space=pl.ANY),
            pl.BlockSpec((TOKEN_TILE, LANES), tok),
            pl.BlockSpec((TOKEN_TILE, d), p_tok),
            pl.BlockSpec((TOKEN_TILE, d), s_tok),
            pl.BlockSpec((TOKEN_TILE, ple), p_tok),
            pl.BlockSpec((TOKEN_TILE, ple), s_tok),
            pl.BlockSpec((1, d), const),
            pl.BlockSpec((d, d), const),
            pl.BlockSpec((ple, d), const),
            pl.BlockSpec((1, d), const),
        ],
        out_specs=(pl.BlockSpec((TOKEN_TILE, d), p_tok), pl.BlockSpec((TOKEN_TILE, d), s_tok)),
        out_shape=(jax.ShapeDtypeStruct((n_p, d), F32), jax.ShapeDtypeStruct((n_s, d), F32)),
        scratch_shapes=[pltpu.VMEM((TOP_K, TOKEN_TILE, d), F32), pltpu.SemaphoreType.DMA(())],
        compiler_params=pltpu.CompilerParams(
            dimension_semantics=("arbitrary",), vmem_limit_bytes=VMEM_LIMIT),
        name="combine",
    )(pos_flat, ys, topw, h1_p, h1_s, p_p, p_s, nple, wg, wp, nfin)


def kernel(x_prompt, x_sample, cache_pool, p_prompt, p_sample, norm_mix, w_in, ln_v_g, ln_v_b, w_spatial, b_spatial, w_pool_group, pool_scale, w_branch_a, w_branch_b, w_out, norm_moe, w_router, b_router, w_up, b_up, w_down, b_down, norm_ple, w_ple_gate, w_ple_proj, norm_final):
    assert norm_mix.shape[0] == 1, "single-layer stack"
    nb, seq, d = x_prompt.shape
    sb, sseq, _ = x_sample.shape
    n_prompt = nb * seq
    n_sample = sb * sseq
    n_tok = n_prompt + n_sample
    past_len = 4096
    row = lambda v: v.reshape(1, -1)

    wr = jnp.pad(w_router[0], ((0, 0), (0, LANES - N_EXPERTS))).astype(BF16)
    br = jnp.pad(b_router[0], (0, LANES - N_EXPERTS)).reshape(1, LANES)
    weights_a = (row(norm_mix[0]), w_in[0].astype(BF16), row(ln_v_g[0]), row(ln_v_b[0]), w_spatial[0],
                 jnp.transpose(b_spatial[0]), w_pool_group[0].astype(BF16), row(pool_scale[0]),
                 w_branch_a[0].astype(BF16), w_branch_b[0].astype(BF16), w_out[0].astype(BF16),
                 row(norm_moe[0]), wr, br)
    h1_p, xn2_p, lg_p, pool_p = _stage_a_prompt(x_prompt, weights_a)
    h1_s, xn2_s, lg_s, pool_s, vn_s = _stage_a_sample(x_sample, cache_pool[0], past_len, weights_a)

    pos_pack, topw, counts = _route(lg_p, lg_s)
    pos_flat = pos_pack[:, :TOP_K].reshape(-1)

    max_tiles = n_tok * TOP_K // ROW_TILE + N_EXPERTS
    tiles_e = (counts[0, :N_EXPERTS].astype(I32) + (ROW_TILE - 1)) // ROW_TILE
    ends = jnp.cumsum(tiles_e)
    num_tiles = ends[-1:]
    tile_ids = jnp.minimum(jnp.arange(max_tiles, dtype=I32), num_tiles[0] - 1)
    tile_expert = jnp.sum((tile_ids[:, None] >= ends[None, :]).astype(I32), axis=1)

    xs = _dispatch(pos_flat, xn2_p, xn2_s, max_tiles * ROW_TILE)
    ys = _experts(tile_expert, num_tiles, xs, w_up[0], b_up[0], w_down[0], b_down[0])

    y_p, y_s = _combine(pos_flat, ys, topw, h1_p, h1_s, p_prompt[0].reshape(n_prompt, -1),
                        p_sample[0].reshape(n_sample, -1), row(norm_ple[0]), w_ple_gate[0].astype(BF16),
                        w_ple_proj[0].astype(BF16), row(norm_final))

    return (y_p.reshape(nb, seq, d), y_s.reshape(sb, sseq, d), pool_p[None], pool_s[None],
            vn_s.reshape(1, sb, sseq, d))
```

```python
import functools

import jax
import jax.numpy as jnp
from jax import lax
from jax.experimental import pallas as pl
from jax.experimental.pallas import tpu as pltpu

F32 = jnp.float32
BF16 = jnp.bfloat16
I32 = jnp.int32

EPS = 1e-6
CHUNK = 64
GMLP_BLOCK = 128
GMLP_HEADS = 4
POOL_WINDOWS = (2, 4, 8, 16)
POOL_STATE = 15
HALO = 16
N_EXPERTS = 32
TOP_K = 4
SWIGLU_LIMIT = 7.0
SWIGLU_ALPHA = 1.702
LANES = 128
SUBLANES = 8

TOKEN_TILE = 256
ROW_TILE = 512
VMEM_LIMIT = 56 * 1024 * 1024


def _rms(xf, g):
    return xf * lax.rsqrt(jnp.mean(xf * xf, axis=-1, keepdims=True) + EPS) * g


def _dot(a, b):
    return jnp.dot(a, b, preferred_element_type=F32)


def _stage_a_kernel(n_seq, seq_rows, carry_halo, pos0, *refs):
    if carry_halo:
        (x_ref, nm_ref, win_ref, lng_ref, lnb_ref, wsp_ref, bsp_ref, wpool_ref, pscale_ref,
         wa_ref, wb_ref, wout_ref, nmoe_ref, wr_ref, br_ref,
         h1_ref, xn2_ref, lg_ref, pool_ref, vn_ref, xc_ref, a_ref) = refs
        cache_ref = None
    else:
        (x_ref, cache_ref, nm_ref, win_ref, lng_ref, lnb_ref, wsp_ref, bsp_ref, wpool_ref,
         pscale_ref, wa_ref, wb_ref, wout_ref, nmoe_ref, wr_ref, br_ref,
         h1_ref, xn2_ref, lg_ref, pool_ref, vn_ref, xc_ref, a_ref) = refs
    d = x_ref.shape[-1]
    head_dim = d // GMLP_HEADS
    group_dim = d // len(POOL_WINDOWS)
    blk = min(GMLP_BLOCK, seq_rows)
    j = pl.program_id(1) if carry_halo else 0

    x = x_ref[...]
    n = _rms(x, nm_ref[...]).astype(BF16)

    def proj(seg):
        return _dot(n, win_ref[:, seg * d:(seg + 1) * d])

    v = jax.nn.gelu(proj(1))
    mu = jnp.mean(v, axis=-1, keepdims=True)
    vc = v - mu
    vn_ref[...] = vc * lax.rsqrt(jnp.mean(vc * vc, axis=-1, keepdims=True) + EPS) * lng_ref[...] + lnb_ref[...]
    u = jax.nn.gelu(proj(0))

    ri = lax.broadcasted_iota(I32, (GMLP_BLOCK, GMLP_BLOCK), 0) // CHUNK
    ci = lax.broadcasted_iota(I32, (GMLP_BLOCK, GMLP_BLOCK), 1) // CHUNK
    causal = (ri >= ci).astype(F32)
    for h in range(GMLP_HEADS):
        wm = (wsp_ref[h] * causal)[:blk, :blk].astype(BF16)
        bias = bsp_ref[:blk, h:h + 1]
        cols = slice(h * head_dim, (h + 1) * head_dim)
        for r0 in range(0, n_seq * seq_rows, blk):
            rows = slice(r0, r0 + blk)
            sg = _dot(wm, vn_ref[rows, cols].astype(BF16)) + bias
            a_ref[rows, cols] = (u[rows, cols] * sg).astype(BF16)
    y_a = _dot(a_ref[...], wa_ref[...])

    pb = proj(2)
    if carry_halo:
        @pl.when(j == 0)
        def _():
            xc_ref[0, 0:HALO, :] = jnp.zeros((HALO, d), F32)
    else:
        for s in range(n_seq):
            xc_ref[s, 0:HALO, :] = jnp.zeros((HALO, d), F32)
            xc_ref[s, HALO - POOL_STATE:HALO, :] = cache_ref[s]
    for s in range(n_seq):
        xc_ref[s, HALO:HALO + seq_rows, :] = pb[s * seq_rows:(s + 1) * seq_rows]

    if carry_halo:
        pos = pos0 + j * seq_rows + lax.broadcasted_iota(I32, (seq_rows, 1), 0)
    z_groups = []
    for g, w in enumerate(POOL_WINDOWS):
        cols = slice(g * group_dim, (g + 1) * group_dim)
        if carry_halo:
            cnt = jnp.minimum(pos + 1, w).astype(F32)
        else:
            cnt = float(min(pos0 + 1, w))
        parts = []
        for s in range(n_seq):
            acc = xc_ref[s, HALO:HALO + seq_rows, cols]
            for k in range(1, w):
                acc = acc + xc_ref[s, HALO - k:HALO - k + seq_rows, cols]
            parts.append(acc / cnt - xc_ref[s, HALO:HALO + seq_rows, cols])
        zg = parts[0] if n_seq == 1 else jnp.concatenate(parts, axis=0)
        z_groups.append(_dot(zg.astype(BF16), wpool_ref[g]) * pscale_ref[:, cols])
    z = jnp.concatenate(z_groups, axis=1)
    y_b = _dot(z.astype(BF16), wb_ref[...])

    for s in range(n_seq):
        tail = xc_ref[s, HALO + seq_rows - POOL_STATE:HALO + seq_rows, :]
        if carry_halo:
            @pl.when(j == pl.num_programs(1) - 1)
            def _():
                pool_ref[...] = tail
        else:
            pool_ref[s] = tail
    if carry_halo:
        xc_ref[0, 0:HALO, :] = xc_ref[0, seq_rows:seq_rows + HALO, :]

    g_a = jax.nn.sigmoid(proj(3))
    g_b = jax.nn.sigmoid(proj(4))
    merged = (g_a * y_a + g_b * y_b).astype(BF16)
    h1 = x + _dot(merged, wout_ref[...])
    h1_ref[...] = h1

    xn2 = _rms(h1, nmoe_ref[...]).astype(BF16)
    lg_ref[...] = _dot(xn2, wr_ref[...]) + br_ref[...]
    xn2_ref[...] = xn2.astype(F32).reshape(xn2_ref.shape)


def _const_spec(shape, grid_rank):
    zeros = (0,) * len(shape)
    if grid_rank == 2:
        return pl.BlockSpec(shape, lambda b, j: zeros, pipeline_mode=pl.Buffered(1))
    return pl.BlockSpec(shape, lambda i: zeros, pipeline_mode=pl.Buffered(1))


def _stage_a_outputs(n_rows, d):
    return (jax.ShapeDtypeStruct((n_rows, d), F32),
            jax.ShapeDtypeStruct((n_rows, SUBLANES, d // SUBLANES), F32),
            jax.ShapeDtypeStruct((n_rows, LANES), F32))


def _stage_a_prompt(x_prompt, weights):
    nb, seq, d = x_prompt.shape
    tiles_per_seq = seq // TOKEN_TILE
    tok_map = lambda b, j: (b * tiles_per_seq + j, 0)
    return pl.pallas_call(
        functools.partial(_stage_a_kernel, 1, TOKEN_TILE, True, 0),
        grid=(nb, tiles_per_seq),
        in_specs=[pl.BlockSpec((None, TOKEN_TILE, d), lambda b, j: (b, j, 0))]
        + [_const_spec(w.shape, 2) for w in weights],
        out_specs=(
            pl.BlockSpec((TOKEN_TILE, d), tok_map),
            pl.BlockSpec((TOKEN_TILE, SUBLANES, d // SUBLANES), lambda b, j: (b * tiles_per_seq + j, 0, 0)),
            pl.BlockSpec((TOKEN_TILE, LANES), tok_map),
            pl.BlockSpec((None, POOL_STATE, d), lambda b, j: (b, 0, 0)),
        ),
        out_shape=_stage_a_outputs(nb * seq, d) + (jax.ShapeDtypeStruct((nb, POOL_STATE, d), F32),),
        scratch_shapes=[
            pltpu.VMEM((TOKEN_TILE, d), F32),
            pltpu.VMEM((1, HALO + TOKEN_TILE, d), F32),
            pltpu.VMEM((TOKEN_TILE, d), BF16),
        ],
        compiler_params=pltpu.CompilerParams(
            dimension_semantics=("arbitrary", "arbitrary"), vmem_limit_bytes=VMEM_LIMIT),
        name="stage_a_prompt",
    )(x_prompt, *weights)


def _stage_a_sample(x_sample, cache, past_len, weights):
    sb, sseq, d = x_sample.shape
    n_s = sb * sseq
    return pl.pallas_call(
        functools.partial(_stage_a_kernel, sb, sseq, False, past_len),
        grid=(1,),
        in_specs=[pl.BlockSpec((n_s, d), lambda i: (0, 0)),
                  pl.BlockSpec((sb, POOL_STATE, d), lambda i: (0, 0, 0))]
        + [_const_spec(w.shape, 1) for w in weights],
        out_specs=(
            pl.BlockSpec((n_s, d), lambda i: (0, 0)),
            pl.BlockSpec((n_s, SUBLANES, d // SUBLANES), lambda i: (0, 0, 0)),
            pl.BlockSpec((n_s, LANES), lambda i: (0, 0)),
            pl.BlockSpec((sb, POOL_STATE, d), lambda i: (0, 0, 0)),
            pl.BlockSpec((n_s, d), lambda i: (0, 0)),
        ),
        out_shape=_stage_a_outputs(n_s, d) + (jax.ShapeDtypeStruct((sb, POOL_STATE, d), F32),
                                              jax.ShapeDtypeStruct((n_s, d), F32)),
        scratch_shapes=[
            pltpu.VMEM((sb, HALO + sseq, d), F32),
            pltpu.VMEM((n_s, d), BF16),
        ],
        compiler_params=pltpu.CompilerParams(
            dimension_semantics=("arbitrary",), vmem_limit_bytes=VMEM_LIMIT),
        name="stage_a_sample",
    )(x_sample.reshape(n_s, d), cache, *weights)


def _route_kernel(lgp_ref, lgs_ref, pos_ref, w_ref, cnt_ref):
    rb = TOKEN_TILE
    n_p, n_s = lgp_ref.shape[0], lgs_ref.shape[0]
    lane = lax.broadcasted_iota(I32, (rb, LANES), 1).astype(F32)
    tri = (lax.broadcasted_iota(I32, (rb, rb), 0) > lax.broadcasted_iota(I32, (rb, rb), 1)).astype(BF16)
    neg = jnp.float32(-jnp.inf)

    def rank_block(lg_ref, row0, i, run):
        src = pl.ds(pl.multiple_of(i * rb, rb), rb)
        dst = pl.ds(pl.multiple_of(row0 + i * rb, rb), rb)
        lg = jnp.where(lane < N_EXPERTS, lg_ref[src, :], neg)
        picked = jnp.zeros((rb, LANES), F32)
        vals, idxs = [], []
        for _ in range(TOP_K):
            m = jnp.max(lg, axis=1, keepdims=True)
            idx = jnp.min(jnp.where(lg == m, lane, float(LANES)), axis=1, keepdims=True)
            sel = lane == idx
            lg = jnp.where(sel, neg, lg)
            picked = picked + sel.astype(F32)
            vals.append(m)
            idxs.append(idx)
        ex = [jnp.exp(v - vals[0]) for v in vals]
        den = ex[0] + ex[1] + ex[2] + ex[3]
        before = _dot(tri, picked.astype(BF16)) + run
        packed = jnp.zeros((rb, LANES), F32)
        wts = jnp.zeros((rb, LANES), F32)
        for k in range(TOP_K):
            rank = jnp.sum(jnp.where(lane == idxs[k], before, 0.0), axis=1, keepdims=True)
            packed = jnp.where(lane == k, rank, packed)
            packed = jnp.where(lane == TOP_K + k, idxs[k], packed)
            wts = jnp.where(lane == k, ex[k] / den, wts)
        pos_ref[dst, :] = packed.astype(I32)
        w_ref[dst, :] = wts
        return run + jnp.sum(picked, axis=0, keepdims=True)

    counts = lax.fori_loop(0, n_p // rb, functools.partial(rank_block, lgp_ref, 0), jnp.zeros((1, LANES), F32))
    counts = lax.fori_loop(0, n_s // rb, functools.partial(rank_block, lgs_ref, n_p), counts)
    cnt_ref[...] = jnp.broadcast_to(counts, cnt_ref.shape)

    tiles = jnp.floor((counts + (ROW_TILE - 1)) * (1.0 / ROW_TILE))
    upper = (lax.broadcasted_iota(I32, (LANES, LANES), 0) < lax.broadcasted_iota(I32, (LANES, LANES), 1)).astype(BF16)
    first = _dot(jnp.broadcast_to(tiles, (8, LANES)).astype(BF16), upper)[0:1, :] * float(ROW_TILE)

    def place_block(i, c):
        rows = pl.ds(pl.multiple_of(i * rb, rb), rb)
        packed = pos_ref[rows, :].astype(F32)
        out = packed
        for k in range(TOP_K):
            e_k = jnp.sum(jnp.where(lane == TOP_K + k, packed, 0.0), axis=1, keepdims=True)
            base = jnp.sum(jnp.where(lane == e_k, first, 0.0), axis=1, keepdims=True)
            out = jnp.where(lane == k, packed + base, out)
        pos_ref[rows, :] = out.astype(I32)
        return c

    lax.fori_loop(0, (n_p + n_s) // rb, place_block, 0)


def _route(lg_p, lg_s):
    n_tok = lg_p.shape[0] + lg_s.shape[0]
    return pl.pallas_call(
        _route_kernel,
        out_shape=(jax.ShapeDtypeStruct((n_tok, LANES), I32),
                   jax.ShapeDtypeStruct((n_tok, LANES), F32),
                   jax.ShapeDtypeStruct((8, LANES), F32)),
        compiler_params=pltpu.CompilerParams(vmem_limit_bytes=VMEM_LIMIT),
        name="route",
    )(lg_p, lg_s)


def _for_tile_rows(fn):
    def body(g, c):
        for rr in range(SUBLANES):
            for k in range(TOP_K):
                fn(g * SUBLANES + rr, k)
        return c

    lax.fori_loop(0, TOKEN_TILE // SUBLANES, body, 0)


def _dispatch_kernel(n_prompt_tiles, pos_ref, pad_lo_ref, pad_hi_ref, nt_ref, xp_ref, xsm_ref, xs_hbm, ztile, sem):
    i = pl.program_id(0)

    @pl.when(i == 0)
    def _():
        ztile[...] = jnp.zeros(ztile.shape, F32)

        def pad_group(e, c):
            def start(r, c2):
                pltpu.make_async_copy(ztile.at[0], xs_hbm.at[r], sem).start()
                return c2

            def wait(r, c2):
                pltpu.make_async_copy(ztile.at[0], xs_hbm.at[r], sem).wait()
                return c2

            lax.fori_loop(pad_lo_ref[e], pad_hi_ref[e], start, 0)
            lax.fori_loop(pad_lo_ref[e], pad_hi_ref[e], wait, 0)
            return c

        lax.fori_loop(0, N_EXPERTS, pad_group, 0)

        def tile_copy(t):
            return pltpu.make_async_copy(ztile, xs_hbm.at[pl.ds(t * ROW_TILE, ROW_TILE)], sem)

        def start_tile(t, c):
            tile_copy(t).start()
            return c

        def wait_tile(t, c):
            tile_copy(t).wait()
            return c

        lax.fori_loop(nt_ref[0], xs_hbm.shape[0] // ROW_TILE, start_tile, 0)
        lax.fori_loop(nt_ref[0], xs_hbm.shape[0] // ROW_TILE, wait_tile, 0)

    def send_tile(src_ref):
        def copy(r, k):
            return pltpu.make_async_copy(src_ref.at[r], xs_hbm.at[pos_ref[r * TOP_K + k]], sem)

        _for_tile_rows(lambda r, k: copy(r, k).start(priority=k % 2))
        _for_tile_rows(lambda r, k: copy(r, k).wait())

    @pl.when(i < n_prompt_tiles)
    def _():
        send_tile(xp_ref)

    @pl.when(i >= n_prompt_tiles)
    def _():
        send_tile(xsm_ref)


def _dispatch(pos_flat, pad_lo, pad_hi, num_tiles, xn2_p, xn2_s, n_rows):
    row_shape = xn2_p.shape[1:]
    npt = xn2_p.shape[0] // TOKEN_TILE
    nst = xn2_s.shape[0] // TOKEN_TILE
    smem_vec = lambda n, imap: pl.BlockSpec((n,), imap, memory_space=pltpu.SMEM)
    return pl.pallas_call(
        functools.partial(_dispatch_kernel, npt),
        grid=(npt + nst,),
        in_specs=[smem_vec(TOKEN_TILE * TOP_K, lambda i: (i,)),
                  smem_vec(N_EXPERTS, lambda i: (0,)),
                  smem_vec(N_EXPERTS, lambda i: (0,)),
                  smem_vec(1, lambda i: (0,)),
                  pl.BlockSpec((TOKEN_TILE,) + row_shape, lambda i: (jnp.minimum(i, npt - 1), 0, 0)),
                  pl.BlockSpec((TOKEN_TILE,) + row_shape, lambda i: (jnp.maximum(i - npt, 0), 0, 0))],
        out_specs=pl.BlockSpec(memory_space=pl.ANY),
        out_shape=jax.ShapeDtypeStruct((n_rows,) + row_shape, F32),
        scratch_shapes=[pltpu.VMEM((ROW_TILE,) + row_shape, F32), pltpu.SemaphoreType.DMA(())],
        compiler_params=pltpu.CompilerParams(dimension_semantics=("arbitrary",)),
        name="dispatch",
    )(pos_flat, pad_lo, pad_hi, num_tiles, xn2_p, xn2_s)


def _expert_kernel(te_ref, nt_ref, xs_ref, wup_ref, bup_ref, wdn_ref, bdn_ref, ys_ref):
    d_ff = wdn_ref.shape[0]
    used = pl.program_id(0) < nt_ref[0]

    @pl.when(used)
    def _():
        x = xs_ref[...].reshape(ROW_TILE, -1).astype(BF16)
        hid = _dot(x, wup_ref[...].astype(BF16)) + bup_ref[...]
        glu = jnp.minimum(hid[:, :d_ff], SWIGLU_LIMIT)
        lin = jnp.clip(hid[:, d_ff:], -SWIGLU_LIMIT, SWIGLU_LIMIT)
        act = glu * jax.nn.sigmoid(SWIGLU_ALPHA * glu) * (lin + 1.0)
        y = _dot(act.astype(BF16), wdn_ref[...].astype(BF16)) + bdn_ref[...]
        ys_ref[...] = y.reshape(ys_ref.shape)

    @pl.when(jnp.logical_not(used))
    def _():
        ys_ref[...] = jnp.zeros(ys_ref.shape, F32)


def _experts(tile_expert, num_tiles, xs, w_up, b_up, w_down, b_down):
    n_rows = xs.shape[0]
    row_shape = xs.shape[1:]
    n_e, d, d_up = w_up.shape
    d_ff = w_down.shape[1]
    exp_map = lambda i, te, nt: (te[i], 0, 0)
    grid_spec = pltpu.PrefetchScalarGridSpec(
        num_scalar_prefetch=2,
        grid=(n_rows // ROW_TILE,),
        in_specs=[
            pl.BlockSpec((ROW_TILE,) + row_shape, lambda i, te, nt: (jnp.minimum(i, nt[0] - 1), 0, 0)),
            pl.BlockSpec((None, d, d_up), exp_map),
            pl.BlockSpec((None, 1, d_up), exp_map),
            pl.BlockSpec((None, d_ff, d), exp_map),
            pl.BlockSpec((None, 1, d), exp_map),
        ],
        out_specs=pl.BlockSpec((ROW_TILE,) + row_shape, lambda i, te, nt: (i, 0, 0)),
    )
    return pl.pallas_call(
        _expert_kernel,
        grid_spec=grid_spec,
        out_shape=jax.ShapeDtypeStruct((n_rows,) + row_shape, F32),
        compiler_params=pltpu.CompilerParams(
            dimension_semantics=("arbitrary",), vmem_limit_bytes=VMEM_LIMIT),
        name="experts",
    )(tile_expert, num_tiles, xs, w_up, b_up.reshape(n_e, 1, d_up), w_down, b_down.reshape(n_e, 1, d))


def _combine_kernel(n_prompt_tiles, pos_ref, pos_next_ref, ys_hbm, w_ref, h1p_ref, h1s_ref, pp_ref, ps_ref,
                    nple_ref, wg_ref, wp_ref, nfin_ref, yp_ref, ysm_ref, gbuf, sems):
    i = pl.program_id(0)
    is_prompt = i < n_prompt_tiles
    slot = i % 2

    def gather(p_ref, s):
        return lambda r, k: pltpu.make_async_copy(ys_hbm.at[p_ref[r * TOP_K + k]], gbuf.at[s, k, r], sems.at[s])

    @pl.when(i == 0)
    def _():
        copy = gather(pos_ref, 0)
        _for_tile_rows(lambda r, k: copy(r, k).start(priority=k % 2))

    @pl.when(i + 1 < pl.num_programs(0))
    def _():
        copy = gather(pos_next_ref, 1 - slot)
        _for_tile_rows(lambda r, k: copy(r, k).start(priority=k % 2))

    copy = gather(pos_ref, slot)
    _for_tile_rows(lambda r, k: copy(r, k).wait())

    d = h1p_ref.shape[-1]
    wts = w_ref[...]
    moe = wts[:, 0:1] * gbuf[slot, 0].reshape(TOKEN_TILE, d)
    for k in range(1, TOP_K):
        moe = moe + wts[:, k:k + 1] * gbuf[slot, k].reshape(TOKEN_TILE, d)
    h2 = jnp.where(is_prompt, h1p_ref[...], h1s_ref[...]) + moe
    p = jnp.where(is_prompt, pp_ref[...], ps_ref[...])
    n3 = _rms(h2, nple_ref[...]).astype(BF16)
    gate = jax.nn.sigmoid(_dot(n3, wg_ref[...]))
    h3 = h2 + gate * _dot(p.astype(BF16), wp_ref[...])
    y = _rms(h3, nfin_ref[...])

    @pl.when(is_prompt)
    def _():
        yp_ref[...] = y

    @pl.when(jnp.logical_not(is_prompt))
    def _():
        ysm_ref[...] = y


def _combine(pos_flat, ys, topw, h1_p, h1_s, p_p, p_s, nple, wg, wp, nfin):
    n_p, d = h1_p.shape
    n_s = h1_s.shape[0]
    ple = p_p.shape[1]
    npt = n_p // TOKEN_TILE
    nst = n_s // TOKEN_TILE
    tok = lambda i: (i, 0)
    p_tok = lambda i: (jnp.minimum(i, npt - 1), 0)
    s_tok = lambda i: (jnp.maximum(i - npt, 0), 0)
    const = lambda i: (0, 0)
    last = npt + nst - 1
    return pl.pallas_call(
        functools.partial(_combine_kernel, npt),
        grid=(npt + nst,),
        in_specs=[
            pl.BlockSpec((TOKEN_TILE * TOP_K,), lambda i: (i,), memory_space=pltpu.SMEM),
            pl.BlockSpec((TOKEN_TILE * TOP_K,), lambda i: (jnp.minimum(i + 1, last),), memory_space=pltpu.SMEM),
            pl.BlockSpec(memory_space=pl.ANY),
            pl.BlockSpec((TOKEN_TILE, LANES), tok),
            pl.BlockSpec((TOKEN_TILE, d), p_tok),
            pl.BlockSpec((TOKEN_TILE, d), s_tok),
            pl.BlockSpec((TOKEN_TILE, ple), p_tok),
            pl.BlockSpec((TOKEN_TILE, ple), s_tok),
            pl.BlockSpec((1, d), const),
            pl.BlockSpec((d, d), const),
            pl.BlockSpec((ple, d), const),
            pl.BlockSpec((1, d), const),
        ],
        out_specs=(pl.BlockSpec((TOKEN_TILE, d), p_tok), pl.BlockSpec((TOKEN_TILE, d), s_tok)),
        out_shape=(jax.ShapeDtypeStruct((n_p, d), F32), jax.ShapeDtypeStruct((n_s, d), F32)),
        scratch_shapes=[pltpu.VMEM((2, TOP_K, TOKEN_TILE) + ys.shape[1:], F32), pltpu.SemaphoreType.DMA((2,))],
        compiler_params=pltpu.CompilerParams(
            dimension_semantics=("arbitrary",), vmem_limit_bytes=VMEM_LIMIT),
        name="combine",
    )(pos_flat, pos_flat, ys, topw, h1_p, h1_s, p_p, p_s, nple, wg, wp, nfin)


def kernel(x_prompt, x_sample, cache_pool, p_prompt, p_sample, norm_mix, w_in, ln_v_g, ln_v_b, w_spatial, b_spatial, w_pool_group, pool_scale, w_branch_a, w_branch_b, w_out, norm_moe, w_router, b_router, w_up, b_up, w_down, b_down, norm_ple, w_ple_gate, w_ple_proj, norm_final):
    assert norm_mix.shape[0] == 1, "single-layer stack"
    nb, seq, d = x_prompt.shape
    sb, sseq, _ = x_sample.shape
    n_prompt = nb * seq
    n_sample = sb * sseq
    n_tok = n_prompt + n_sample
    past_len = 4096
    row = lambda v: v.reshape(1, -1)

    wr = jnp.pad(w_router[0], ((0, 0), (0, LANES - N_EXPERTS))).astype(BF16)
    br = jnp.pad(b_router[0], (0, LANES - N_EXPERTS)).reshape(1, LANES)
    weights_a = (row(norm_mix[0]), w_in[0].astype(BF16), row(ln_v_g[0]), row(ln_v_b[0]), w_spatial[0],
                 jnp.transpose(b_spatial[0]), w_pool_group[0].astype(BF16), row(pool_scale[0]),
                 w_branch_a[0].astype(BF16), w_branch_b[0].astype(BF16), w_out[0].astype(BF16),
                 row(norm_moe[0]), wr, br)
    h1_p, xn2_p, lg_p, pool_p = _stage_a_prompt(x_prompt, weights_a)
    h1_s, xn2_s, lg_s, pool_s, vn_s = _stage_a_sample(x_sample, cache_pool[0], past_len, weights_a)

    pos_pack, topw, counts = _route(lg_p, lg_s)
    pos_flat = pos_pack[:, :TOP_K].reshape(-1)

    max_tiles = n_tok * TOP_K // ROW_TILE + N_EXPERTS
    tiles_e = (counts[0, :N_EXPERTS].astype(I32) + (ROW_TILE - 1)) // ROW_TILE
    ends = jnp.cumsum(tiles_e)
    num_tiles = ends[-1:]
    tile_ids = jnp.minimum(jnp.arange(max_tiles, dtype=I32), num_tiles[0] - 1)
    tile_expert = jnp.sum((tile_ids[:, None] >= ends[None, :]).astype(I32), axis=1)
    pad_hi = ends * ROW_TILE
    pad_lo = pad_hi - tiles_e * ROW_TILE + counts[0, :N_EXPERTS].astype(I32)

    xs = _dispatch(pos_flat, pad_lo, pad_hi, num_tiles, xn2_p, xn2_s, max_tiles * ROW_TILE)
    ys = _experts(tile_expert, num_tiles, xs, w_up[0], b_up[0], w_down[0], b_down[0])

    y_p, y_s = _combine(pos_flat, ys, topw, h1_p, h1_s, p_prompt[0].reshape(n_prompt, -1),
                        p_sample[0].reshape(n_sample, -1), row(norm_ple[0]), w_ple_gate[0].astype(BF16),
                        w_ple_proj[0].astype(BF16), row(norm_final))

    return (y_p.reshape(nb, seq, d), y_s.reshape(sb, sseq, d), pool_p[None], pool_s[None],
            vn_s.reshape(1, sb, sseq, d))
```

```python
import functools

import jax
import jax.numpy as jnp
from jax import lax
from jax.experimental import pallas as pl
from jax.experimental.pallas import tpu as pltpu

F32 = jnp.float32
BF16 = jnp.bfloat16
I32 = jnp.int32

EPS = 1e-6
CHUNK = 64
GMLP_BLOCK = 128
GMLP_HEADS = 4
POOL_WINDOWS = (2, 4, 8, 16)
POOL_STATE = 15
HALO = 16
N_EXPERTS = 32
TOP_K = 4
SWIGLU_LIMIT = 7.0
SWIGLU_ALPHA = 1.702
LANES = 128
SUBLANES = 8

TOKEN_TILE = 256
ROW_TILE = 512
RUN_WINDOW = 16
LOCAL_ROWS = N_EXPERTS * RUN_WINDOW + TOKEN_TILE * TOP_K
ZERO_ROWS = 64
VMEM_LIMIT = 56 * 1024 * 1024


def _rms(xf, g):
    return xf * lax.rsqrt(jnp.mean(xf * xf, axis=-1, keepdims=True) + EPS) * g


def _dot(a, b):
    return jnp.dot(a, b, preferred_element_type=F32)


ROUTE_ROWS = 4 * TOP_K
WEIGHT_ROWS = 2 * TOP_K


def _route_tile(logits):
    ne, nt = logits.shape
    eid = lax.broadcasted_iota(I32, (ne, nt), 0).astype(F32)
    neg = jnp.float32(-jnp.inf)
    lg = logits
    picked = jnp.zeros((ne, nt), F32)
    vals, sels, idxs = [], [], []
    for _ in range(TOP_K):
        m = jnp.max(lg, axis=0, keepdims=True)
        idx = jnp.min(jnp.where(lg == m, eid, float(ne)), axis=0, keepdims=True)
        sel = eid == idx
        lg = jnp.where(sel, neg, lg)
        picked = picked + sel.astype(F32)
        vals.append(m)
        sels.append(sel)
        idxs.append(idx)
    ex = [jnp.exp(v - vals[0]) for v in vals]
    den = ex[0] + ex[1] + ex[2] + ex[3]

    earlier = (lax.broadcasted_iota(I32, (nt, nt), 0) < lax.broadcasted_iota(I32, (nt, nt), 1)).astype(BF16)
    before = _dot(picked.astype(BF16), earlier)
    counts = jnp.sum(picked, axis=1, keepdims=True)
    windows = jnp.floor((counts + (RUN_WINDOW - 1)) * (1.0 / RUN_WINDOW))
    lower = (lax.broadcasted_iota(I32, (ne, ne), 0) > lax.broadcasted_iota(I32, (ne, ne), 1)).astype(BF16)
    run_start = _dot(lower, jnp.broadcast_to(windows, (ne, LANES)).astype(BF16))[:, 0:1] * float(RUN_WINDOW)

    zero_row = jnp.zeros((1, nt), F32)
    ranks = [jnp.sum(jnp.where(s, before, 0.0), axis=0, keepdims=True) for s in sels]
    places = [jnp.sum(jnp.where(s, run_start + before, 0.0), axis=0, keepdims=True) for s in sels]
    packed = jnp.concatenate(places + idxs + ranks + [zero_row] * (ROUTE_ROWS - 3 * TOP_K), axis=0)
    wts = jnp.concatenate([e / den for e in ex] + [zero_row] * (WEIGHT_ROWS - TOP_K), axis=0)
    return packed.astype(I32), wts, counts


def _stage_a_kernel(n_seq, seq_rows, carry_halo, pos0, *refs):
    if carry_halo:
        (x_ref, nm_ref, win_ref, lng_ref, lnb_ref, wsp_ref, bsp_ref, wpool_ref, pscale_ref,
         wa_ref, wb_ref, wout_ref, nmoe_ref, wr_ref, br_ref,
         h1_ref, xn2_ref, route_ref, rw_ref, cnt_ref, pool_ref, vn_ref, xc_ref, a_ref) = refs
        cache_ref = None
    else:
        (x_ref, cache_ref, nm_ref, win_ref, lng_ref, lnb_ref, wsp_ref, bsp_ref, wpool_ref,
         pscale_ref, wa_ref, wb_ref, wout_ref, nmoe_ref, wr_ref, br_ref,
         h1_ref, xn2_ref, route_ref, rw_ref, cnt_ref, pool_ref, vn_ref, xc_ref, a_ref) = refs
    d = x_ref.shape[-1]
    head_dim = d // GMLP_HEADS
    group_dim = d // len(POOL_WINDOWS)
    blk = min(GMLP_BLOCK, seq_rows)
    j = pl.program_id(1) if carry_halo else 0

    x = x_ref[...]
    n = _rms(x, nm_ref[...]).astype(BF16)

    def proj(seg):
        return _dot(n, win_ref[:, seg * d:(seg + 1) * d])

    v = jax.nn.gelu(proj(1))
    mu = jnp.mean(v, axis=-1, keepdims=True)
    vc = v - mu
    vn_ref[...] = vc * lax.rsqrt(jnp.mean(vc * vc, axis=-1, keepdims=True) + EPS) * lng_ref[...] + lnb_ref[...]
    u = jax.nn.gelu(proj(0))

    ri = lax.broadcasted_iota(I32, (GMLP_BLOCK, GMLP_BLOCK), 0) // CHUNK
    ci = lax.broadcasted_iota(I32, (GMLP_BLOCK, GMLP_BLOCK), 1) // CHUNK
    causal = (ri >= ci).astype(F32)
    for h in range(GMLP_HEADS):
        wm = (wsp_ref[h] * causal)[:blk, :blk].astype(BF16)
        bias = bsp_ref[:blk, h:h + 1]
        cols = slice(h * head_dim, (h + 1) * head_dim)
        for r0 in range(0, n_seq * seq_rows, blk):
            rows = slice(r0, r0 + blk)
            sg = _dot(wm, vn_ref[rows, cols].astype(BF16)) + bias
            a_ref[rows, cols] = (u[rows, cols] * sg).astype(BF16)
    y_a = _dot(a_ref[...], wa_ref[...])

    pb = proj(2)
    if carry_halo:
        @pl.when(j == 0)
        def _():
            xc_ref[0, 0:HALO, :] = jnp.zeros((HALO, d), F32)
    else:
        for s in range(n_seq):
            xc_ref[s, 0:HALO, :] = jnp.zeros((HALO, d), F32)
            xc_ref[s, HALO - POOL_STATE:HALO, :] = cache_ref[s]
    for s in range(n_seq):
        xc_ref[s, HALO:HALO + seq_rows, :] = pb[s * seq_rows:(s + 1) * seq_rows]

    if carry_halo:
        pos = pos0 + j * seq_rows + lax.broadcasted_iota(I32, (seq_rows, 1), 0)
    z_groups = []
    for g, w in enumerate(POOL_WINDOWS):
        cols = slice(g * group_dim, (g + 1) * group_dim)
        if carry_halo:
            cnt = jnp.minimum(pos + 1, w).astype(F32)
        else:
            cnt = float(min(pos0 + 1, w))
        parts = []
        for s in range(n_seq):
            acc = xc_ref[s, HALO:HALO + seq_rows, cols]
            for k in range(1, w):
                acc = acc + xc_ref[s, HALO - k:HALO - k + seq_rows, cols]
            parts.append(acc / cnt - xc_ref[s, HALO:HALO + seq_rows, cols])
        zg = parts[0] if n_seq == 1 else jnp.concatenate(parts, axis=0)
        z_groups.append(_dot(zg.astype(BF16), wpool_ref[g]) * pscale_ref[:, cols])
    z = jnp.concatenate(z_groups, axis=1)
    y_b = _dot(z.astype(BF16), wb_ref[...])

    for s in range(n_seq):
        tail = xc_ref[s, HALO + seq_rows - POOL_STATE:HALO + seq_rows, :]
        if carry_halo:
            @pl.when(j == pl.num_programs(1) - 1)
            def _():
                pool_ref[...] = tail
        else:
            pool_ref[s] = tail
    if carry_halo:
        xc_ref[0, 0:HALO, :] = xc_ref[0, seq_rows:seq_rows + HALO, :]

    g_a = jax.nn.sigmoid(proj(3))
    g_b = jax.nn.sigmoid(proj(4))
    merged = (g_a * y_a + g_b * y_b).astype(BF16)
    h1 = x + _dot(merged, wout_ref[...])
    h1_ref[...] = h1

    xn2 = _rms(h1, nmoe_ref[...]).astype(BF16)
    xn2_ref[...] = xn2.astype(F32).reshape(xn2_ref.shape)
    for t in range(n_seq * seq_rows // TOKEN_TILE):
        rows = slice(t * TOKEN_TILE, (t + 1) * TOKEN_TILE)
        logits = lax.dot_general(wr_ref[...], xn2[rows], (((1,), (1,)), ((), ())),
                                 preferred_element_type=F32) + br_ref[...]
        packed, wts, counts = _route_tile(logits)
        route_ref[t] = packed
        rw_ref[t] = wts
        cnt_ref[t] = jnp.broadcast_to(counts, (N_EXPERTS, LANES))


def _const_spec(shape, grid_rank):
    zeros = (0,) * len(shape)
    if grid_rank == 2:
        return pl.BlockSpec(shape, lambda b, j: zeros, pipeline_mode=pl.Buffered(1))
    return pl.BlockSpec(shape, lambda i: zeros, pipeline_mode=pl.Buffered(1))


def _stage_a_outputs(n_rows, d):
    return (jax.ShapeDtypeStruct((n_rows, d), F32),
            jax.ShapeDtypeStruct((n_rows, SUBLANES, d // SUBLANES), F32),
            jax.ShapeDtypeStruct((n_rows // TOKEN_TILE, ROUTE_ROWS, TOKEN_TILE), I32),
            jax.ShapeDtypeStruct((n_rows // TOKEN_TILE, WEIGHT_ROWS, TOKEN_TILE), F32),
            jax.ShapeDtypeStruct((n_rows // TOKEN_TILE, N_EXPERTS, LANES), F32))


def _stage_a_prompt(x_prompt, weights):
    nb, seq, d = x_prompt.shape
    tiles_per_seq = seq // TOKEN_TILE
    tok_map = lambda b, j: (b * tiles_per_seq + j, 0)
    tok_map3 = lambda b, j: (b * tiles_per_seq + j, 0, 0)
    return pl.pallas_call(
        functools.partial(_stage_a_kernel, 1, TOKEN_TILE, True, 0),
        grid=(nb, tiles_per_seq),
        in_specs=[pl.BlockSpec((None, TOKEN_TILE, d), lambda b, j: (b, j, 0))]
        + [_const_spec(w.shape, 2) for w in weights],
        out_specs=(
            pl.BlockSpec((TOKEN_TILE, d), tok_map),
            pl.BlockSpec((TOKEN_TILE, SUBLANES, d // SUBLANES), tok_map3),
            pl.BlockSpec((1, ROUTE_ROWS, TOKEN_TILE), tok_map3),
            pl.BlockSpec((1, WEIGHT_ROWS, TOKEN_TILE), tok_map3),
            pl.BlockSpec((1, N_EXPERTS, LANES), tok_map3),
            pl.BlockSpec((None, POOL_STATE, d), lambda b, j: (b, 0, 0)),
        ),
        out_shape=_stage_a_outputs(nb * seq, d) + (jax.ShapeDtypeStruct((nb, POOL_STATE, d), F32),),
        scratch_shapes=[
            pltpu.VMEM((TOKEN_TILE, d), F32),
            pltpu.VMEM((1, HALO + TOKEN_TILE, d), F32),
            pltpu.VMEM((TOKEN_TILE, d), BF16),
        ],
        compiler_params=pltpu.CompilerParams(
            dimension_semantics=("arbitrary", "arbitrary"), vmem_limit_bytes=VMEM_LIMIT),
        name="stage_a_prompt",
    )(x_prompt, *weights)


def _stage_a_sample(x_sample, cache, past_len, weights):
    sb, sseq, d = x_sample.shape
    n_s = sb * sseq
    whole2 = lambda i: (0, 0)
    whole3 = lambda i: (0, 0, 0)
    return pl.pallas_call(
        functools.partial(_stage_a_kernel, sb, sseq, False, past_len),
        grid=(1,),
        in_specs=[pl.BlockSpec((n_s, d), whole2), pl.BlockSpec((sb, POOL_STATE, d), whole3)]
        + [_const_spec(w.shape, 1) for w in weights],
        out_specs=(
            pl.BlockSpec((n_s, d), whole2),
            pl.BlockSpec((n_s, SUBLANES, d // SUBLANES), whole3),
            pl.BlockSpec((n_s // TOKEN_TILE, ROUTE_ROWS, TOKEN_TILE), whole3),
            pl.BlockSpec((n_s // TOKEN_TILE, WEIGHT_ROWS, TOKEN_TILE), whole3),
            pl.BlockSpec((n_s // TOKEN_TILE, N_EXPERTS, LANES), whole3),
            pl.BlockSpec((sb, POOL_STATE, d), whole3),
            pl.BlockSpec((n_s, d), whole2),
        ),
        out_shape=_stage_a_outputs(n_s, d) + (jax.ShapeDtypeStruct((sb, POOL_STATE, d), F32),
                                              jax.ShapeDtypeStruct((n_s, d), F32)),
        scratch_shapes=[
            pltpu.VMEM((sb, HALO + sseq, d), F32),
            pltpu.VMEM((n_s, d), BF16),
        ],
        compiler_params=pltpu.CompilerParams(
            dimension_semantics=("arbitrary",), vmem_limit_bytes=VMEM_LIMIT),
        name="stage_a_sample",
    )(x_sample.reshape(n_s, d), cache, *weights)


def _for_tile_rows(fn):
    def body(g, c):
        for rr in range(SUBLANES):
            for k in range(TOP_K):
                fn(g * SUBLANES + rr, k)
        return c

    lax.fori_loop(0, TOKEN_TILE // SUBLANES, body, 0)


def _loop(lo, hi, fn):
    def body(i, c):
        fn(i)
        return c

    lax.fori_loop(lo, hi, body, 0)


def _dispatch_kernel(n_prompt_tiles, lpos_ref, run_dst_ref, run_src_ref, run_win_ref, pad_lo_ref, pad_hi_ref,
                     nt_ref, xp_ref, xsm_ref, xs_hbm, lbuf, zbuf, sems):
    i = pl.program_id(0)
    last = pl.num_programs(0) - 1
    slot = i % 2

    @pl.when(i == 0)
    def _():
        lbuf[...] = jnp.zeros(lbuf.shape, F32)
        zbuf[...] = jnp.zeros(zbuf.shape, F32)
        pad_sem = sems.at[0]

        def pad_group(e):
            lo, hi = pad_lo_ref[e], pad_hi_ref[e]
            n_big = (hi - lo) // ZERO_ROWS
            mid = lo + n_big * ZERO_ROWS
            n_mid = (hi - mid) // SUBLANES
            fine = mid + n_mid * SUBLANES
            big = lambda c: pltpu.make_async_copy(zbuf, xs_hbm.at[pl.ds(lo + c * ZERO_ROWS, ZERO_ROWS)], pad_sem)
            med = lambda c: pltpu.make_async_copy(zbuf.at[pl.ds(0, SUBLANES)],
                                                  xs_hbm.at[pl.ds(mid + c * SUBLANES, SUBLANES)], pad_sem)
            one = lambda r: pltpu.make_async_copy(zbuf.at[0], xs_hbm.at[r], pad_sem)
            _loop(0, n_big, lambda c: big(c).start())
            _loop(0, n_mid, lambda c: med(c).start())
            _loop(fine, hi, lambda r: one(r).start())
            _loop(0, n_big, lambda c: big(c).wait())
            _loop(0, n_mid, lambda c: med(c).wait())
            _loop(fine, hi, lambda r: one(r).wait())

        _loop(0, N_EXPERTS, pad_group)

        n_chunks = ROW_TILE // ZERO_ROWS
        chunk = lambda c: pltpu.make_async_copy(zbuf, xs_hbm.at[pl.ds(c * ZERO_ROWS, ZERO_ROWS)], pad_sem)
        _loop(nt_ref[0] * n_chunks, (xs_hbm.shape[0] // ROW_TILE) * n_chunks, lambda c: chunk(c).start())
        _loop(nt_ref[0] * n_chunks, (xs_hbm.shape[0] // ROW_TILE) * n_chunks, lambda c: chunk(c).wait())

    def place(src_ref):
        def body(r, k):
            lbuf[slot, lpos_ref[r * TOP_K + k]] = src_ref[r]
        _for_tile_rows(body)

    @pl.when(i < n_prompt_tiles)
    def _():
        place(xp_ref)

    @pl.when(i >= n_prompt_tiles)
    def _():
        place(xsm_ref)

    def run_copies(tile, s, fn):
        def per_expert(e):
            t = tile * N_EXPERTS + e
            src0, dst0 = run_src_ref[t], run_dst_ref[t]

            def per_window(w):
                fn(pltpu.make_async_copy(lbuf.at[s, pl.ds(src0 + w * RUN_WINDOW, RUN_WINDOW)],
                                         xs_hbm.at[pl.ds(dst0 + w * RUN_WINDOW, RUN_WINDOW)], sems.at[s]))

            _loop(0, run_win_ref[t], per_window)

        _loop(0, N_EXPERTS, per_expert)

    @pl.when(i > 0)
    def _():
        run_copies(i - 1, 1 - slot, lambda c: c.wait())

    run_copies(i, slot, lambda c: c.start())

    @pl.when(i == last)
    def _():
        run_copies(i, slot, lambda c: c.wait())


def _dispatch(lpos_flat, run_dst, run_src, run_win, pad_lo, pad_hi, num_tiles, xn2_p, xn2_s, n_rows):
    row_shape = xn2_p.shape[1:]
    npt = xn2_p.shape[0] // TOKEN_TILE
    nst = xn2_s.shape[0] // TOKEN_TILE
    smem_all = lambda a: pl.BlockSpec(a.shape, lambda i: (0,), memory_space=pltpu.SMEM)
    return pl.pallas_call(
        functools.partial(_dispatch_kernel, npt),
        grid=(npt + nst,),
        in_specs=[pl.BlockSpec((TOKEN_TILE * TOP_K,), lambda i: (i,), memory_space=pltpu.SMEM),
                  smem_all(run_dst), smem_all(run_src), smem_all(run_win),
                  smem_all(pad_lo), smem_all(pad_hi), smem_all(num_tiles),
                  pl.BlockSpec((TOKEN_TILE,) + row_shape, lambda i: (jnp.minimum(i, npt - 1), 0, 0)),
                  pl.BlockSpec((TOKEN_TILE,) + row_shape, lambda i: (jnp.maximum(i - npt, 0), 0, 0))],
        out_specs=pl.BlockSpec(memory_space=pl.ANY),
        out_shape=jax.ShapeDtypeStruct((n_rows,) + row_shape, F32),
        scratch_shapes=[pltpu.VMEM((2, LOCAL_ROWS) + row_shape, F32),
                        pltpu.VMEM((ZERO_ROWS,) + row_shape, F32),
                        pltpu.SemaphoreType.DMA((2,))],
        compiler_params=pltpu.CompilerParams(
            dimension_semantics=("arbitrary",), vmem_limit_bytes=VMEM_LIMIT),
        name="dispatch",
    )(lpos_flat, run_dst, run_src, run_win, pad_lo, pad_hi, num_tiles, xn2_p, xn2_s)


def _expert_kernel(te_ref, nt_ref, xs_ref, wup_ref, bup_ref, wdn_ref, bdn_ref, ys_ref):
    d_ff = wdn_ref.shape[0]
    used = pl.program_id(0) < nt_ref[0]

    @pl.when(used)
    def _():
        x = xs_ref[...].reshape(ROW_TILE, -1).astype(BF16)
        hid = _dot(x, wup_ref[...].astype(BF16)) + bup_ref[...]
        glu = jnp.minimum(hid[:, :d_ff], SWIGLU_LIMIT)
        lin = jnp.clip(hid[:, d_ff:], -SWIGLU_LIMIT, SWIGLU_LIMIT)
        act = glu * jax.nn.sigmoid(SWIGLU_ALPHA * glu) * (lin + 1.0)
        y = _dot(act.astype(BF16), wdn_ref[...].astype(BF16)) + bdn_ref[...]
        ys_ref[...] = y.reshape(ys_ref.shape)

    @pl.when(jnp.logical_not(used))
    def _():
        ys_ref[...] = jnp.zeros(ys_ref.shape, F32)


def _experts(tile_expert, num_tiles, xs, w_up, b_up, w_down, b_down):
    n_rows = xs.shape[0]
    row_shape = xs.shape[1:]
    n_e, d, d_up = w_up.shape
    d_ff = w_down.shape[1]
    exp_map = lambda i, te, nt: (te[i], 0, 0)
    grid_spec = pltpu.PrefetchScalarGridSpec(
        num_scalar_prefetch=2,
        grid=(n_rows // ROW_TILE,),
        in_specs=[
            pl.BlockSpec((ROW_TILE,) + row_shape, lambda i, te, nt: (jnp.minimum(i, nt[0] - 1), 0, 0)),
            pl.BlockSpec((None, d, d_up), exp_map),
            pl.BlockSpec((None, 1, d_up), exp_map),
            pl.BlockSpec((None, d_ff, d), exp_map),
            pl.BlockSpec((None, 1, d), exp_map),
        ],
        out_specs=pl.BlockSpec((ROW_TILE,) + row_shape, lambda i, te, nt: (i, 0, 0)),
    )
    return pl.pallas_call(
        _expert_kernel,
        grid_spec=grid_spec,
        out_shape=jax.ShapeDtypeStruct((n_rows,) + row_shape, F32),
        compiler_params=pltpu.CompilerParams(
            dimension_semantics=("arbitrary",), vmem_limit_bytes=VMEM_LIMIT),
        name="experts",
    )(tile_expert, num_tiles, xs, w_up, b_up.reshape(n_e, 1, d_up), w_down, b_down.reshape(n_e, 1, d))


def _combine_kernel(n_prompt_tiles, exp_ref, rank_ref, exp_next_ref, rank_next_ref, run_dst_ref, ys_hbm,
                    w_ref, h1p_ref, h1s_ref, pp_ref, ps_ref, nple_ref, wg_ref, wp_ref, nfin_ref,
                    yp_ref, ysm_ref, gbuf, sems):
    i = pl.program_id(0)
    is_prompt = i < n_prompt_tiles
    slot = i % 2

    def gather(tile, e_ref, r_ref, s):
        def copy(r, k):
            row = run_dst_ref[tile * N_EXPERTS + e_ref[r * TOP_K + k]] + r_ref[r * TOP_K + k]
            return pltpu.make_async_copy(ys_hbm.at[row], gbuf.at[s, k, r], sems.at[s])
        return copy

    @pl.when(i == 0)
    def _():
        copy = gather(i, exp_ref, rank_ref, 0)
        _for_tile_rows(lambda r, k: copy(r, k).start(priority=k % 2))

    @pl.when(i + 1 < pl.num_programs(0))
    def _():
        copy = gather(i + 1, exp_next_ref, rank_next_ref, 1 - slot)
        _for_tile_rows(lambda r, k: copy(r, k).start(priority=k % 2))

    copy = gather(i, exp_ref, rank_ref, slot)
    _for_tile_rows(lambda r, k: copy(r, k).wait())

    d = h1p_ref.shape[-1]
    wts = w_ref[...]
    moe = wts[:, 0:1] * gbuf[slot, 0].reshape(TOKEN_TILE, d)
    for k in range(1, TOP_K):
        moe = moe + wts[:, k:k + 1] * gbuf[slot, k].reshape(TOKEN_TILE, d)
    h2 = jnp.where(is_prompt, h1p_ref[...], h1s_ref[...]) + moe
    p = jnp.where(is_prompt, pp_ref[...], ps_ref[...])
    n3 = _rms(h2, nple_ref[...]).astype(BF16)
    gate = jax.nn.sigmoid(_dot(n3, wg_ref[...]))
    h3 = h2 + gate * _dot(p.astype(BF16), wp_ref[...])
    y = _rms(h3, nfin_ref[...])

    @pl.when(is_prompt)
    def _():
        yp_ref[...] = y

    @pl.when(jnp.logical_not(is_prompt))
    def _():
        ysm_ref[...] = y


def _combine(exp_flat, rank_flat, run_dst, ys, topw, h1_p, h1_s, p_p, p_s, nple, wg, wp, nfin):
    n_p, d = h1_p.shape
    n_s = h1_s.shape[0]
    ple = p_p.shape[1]
    npt = n_p // TOKEN_TILE
    nst = n_s // TOKEN_TILE
    p_tok = lambda i: (jnp.minimum(i, npt - 1), 0)
    s_tok = lambda i: (jnp.maximum(i - npt, 0), 0)
    const = lambda i: (0, 0)
    last = npt + nst - 1
    this_tile = pl.BlockSpec((TOKEN_TILE * TOP_K,), lambda i: (i,), memory_space=pltpu.SMEM)
    next_tile = pl.BlockSpec((TOKEN_TILE * TOP_K,), lambda i: (jnp.minimum(i + 1, last),), memory_space=pltpu.SMEM)
    return pl.pallas_call(
        functools.partial(_combine_kernel, npt),
        grid=(npt + nst,),
        in_specs=[
            this_tile, this_tile, next_tile, next_tile,
            pl.BlockSpec(run_dst.shape, lambda i: (0,), memory_space=pltpu.SMEM),
            pl.BlockSpec(memory_space=pl.ANY),
            pl.BlockSpec((TOKEN_TILE, TOP_K), lambda i: (i, 0)),
            pl.BlockSpec((TOKEN_TILE, d), p_tok),
            pl.BlockSpec((TOKEN_TILE, d), s_tok),
            pl.BlockSpec((TOKEN_TILE, ple), p_tok),
            pl.BlockSpec((TOKEN_TILE, ple), s_tok),
            pl.BlockSpec((1, d), const),
            pl.BlockSpec((d, d), const),
            pl.BlockSpec((ple, d), const),
            pl.BlockSpec((1, d), const),
        ],
        out_specs=(pl.BlockSpec((TOKEN_TILE, d), p_tok), pl.BlockSpec((TOKEN_TILE, d), s_tok)),
        out_shape=(jax.ShapeDtypeStruct((n_p, d), F32), jax.ShapeDtypeStruct((n_s, d), F32)),
        scratch_shapes=[pltpu.VMEM((2, TOP_K, TOKEN_TILE) + ys.shape[1:], F32), pltpu.SemaphoreType.DMA((2,))],
        compiler_params=pltpu.CompilerParams(
            dimension_semantics=("arbitrary",), vmem_limit_bytes=VMEM_LIMIT),
        name="combine",
    )(exp_flat, rank_flat, exp_flat, rank_flat, run_dst, ys, topw, h1_p, h1_s, p_p, p_s, nple, wg, wp, nfin)


def kernel(x_prompt, x_sample, cache_pool, p_prompt, p_sample, norm_mix, w_in, ln_v_g, ln_v_b, w_spatial, b_spatial, w_pool_group, pool_scale, w_branch_a, w_branch_b, w_out, norm_moe, w_router, b_router, w_up, b_up, w_down, b_down, norm_ple, w_ple_gate, w_ple_proj, norm_final):
    assert norm_mix.shape[0] == 1, "single-layer stack"
    nb, seq, d = x_prompt.shape
    sb, sseq, _ = x_sample.shape
    n_prompt = nb * seq
    n_sample = sb * sseq
    n_tok = n_prompt + n_sample
    past_len = 4096
    row = lambda v: v.reshape(1, -1)

    wr = jnp.transpose(w_router[0]).astype(BF16)
    br = b_router[0].reshape(N_EXPERTS, 1)
    weights_a = (row(norm_mix[0]), w_in[0].astype(BF16), row(ln_v_g[0]), row(ln_v_b[0]), w_spatial[0],
                 jnp.transpose(b_spatial[0]), w_pool_group[0].astype(BF16), row(pool_scale[0]),
                 w_branch_a[0].astype(BF16), w_branch_b[0].astype(BF16), w_out[0].astype(BF16),
                 row(norm_moe[0]), wr, br)
    h1_p, xn2_p, route_p, rw_p, cnt_p, pool_p = _stage_a_prompt(x_prompt, weights_a)
    h1_s, xn2_s, route_s, rw_s, cnt_s, pool_s, vn_s = _stage_a_sample(x_sample, cache_pool[0], past_len, weights_a)

    n_tab = jnp.concatenate([cnt_p[:, :, 0], cnt_s[:, :, 0]], axis=0).astype(I32)
    ahead = jnp.cumsum(n_tab, axis=0) - n_tab
    total = jnp.sum(n_tab, axis=0)
    tiles_e = (total + (RUN_WINDOW - 1) + (ROW_TILE - 1)) // ROW_TILE
    ends = jnp.cumsum(tiles_e)
    first = (ends - tiles_e) * ROW_TILE
    num_tiles = ends[-1:]
    max_tiles = (n_tok * TOP_K + N_EXPERTS * (RUN_WINDOW - 1)) // ROW_TILE + N_EXPERTS
    tile_ids = jnp.minimum(jnp.arange(max_tiles, dtype=I32), num_tiles[0] - 1)
    tile_expert = jnp.sum((tile_ids[:, None] >= ends[None, :]).astype(I32), axis=1)
    run_dst = (first[None, :] + ahead).reshape(-1)
    run_win = (n_tab + (RUN_WINDOW - 1)) // RUN_WINDOW
    run_src = ((jnp.cumsum(run_win, axis=1) - run_win) * RUN_WINDOW).reshape(-1)
    pad_lo = first + total
    pad_hi = ends * ROW_TILE

    per_token = lambda a: jnp.transpose(a, (0, 2, 1)).reshape(n_tok, TOP_K)
    route = jnp.concatenate([route_p, route_s], axis=0)
    lpos_flat = per_token(route[:, :TOP_K]).reshape(-1)
    exp_flat = per_token(route[:, TOP_K:2 * TOP_K]).reshape(-1)
    rank_flat = per_token(route[:, 2 * TOP_K:3 * TOP_K]).reshape(-1)
    topw = per_token(jnp.concatenate([rw_p, rw_s], axis=0)[:, :TOP_K])

    xs = _dispatch(lpos_flat, run_dst, run_src, run_win.reshape(-1), pad_lo, pad_hi, num_tiles,
                   xn2_p, xn2_s, max_tiles * ROW_TILE)
    ys = _experts(tile_expert, num_tiles, xs, w_up[0], b_up[0], w_down[0], b_down[0])

    y_p, y_s = _combine(exp_flat, rank_flat, run_dst, ys, topw, h1_p, h1_s, p_prompt[0].reshape(n_prompt, -1),
                        p_sample[0].reshape(n_sample, -1), row(norm_ple[0]), w_ple_gate[0].astype(BF16),
                        w_ple_proj[0].astype(BF16), row(norm_final))

    return (y_p.reshape(nb, seq, d), y_s.reshape(sb, sseq, d), pool_p[None], pool_s[None],
            vn_s.reshape(1, sb, sseq, d))
```

```python
import functools

import jax
import jax.numpy as jnp
from jax import lax
from jax.experimental import pallas as pl
from jax.experimental.pallas import tpu as pltpu

F32 = jnp.float32
BF16 = jnp.bfloat16
I32 = jnp.int32

EPS = 1e-6
CHUNK = 64
GMLP_BLOCK = 128
GMLP_HEADS = 4
POOL_WINDOWS = (2, 4, 8, 16)
POOL_STATE = 15
HALO = 16
N_EXPERTS = 32
TOP_K = 4
SWIGLU_LIMIT = 7.0
SWIGLU_ALPHA = 1.702
LANES = 128
SUBLANES = 8

TOKEN_TILE = 256
ROW_TILE = 512
RUN_WINDOW = 16
LOCAL_ROWS = N_EXPERTS * RUN_WINDOW + TOKEN_TILE * TOP_K
ZERO_ROWS = 64
VMEM_LIMIT = 56 * 1024 * 1024


def _rms(xf, g):
    return xf * lax.rsqrt(jnp.mean(xf * xf, axis=-1, keepdims=True) + EPS) * g


def _dot(a, b):
    return jnp.dot(a, b, preferred_element_type=F32)


ROUTE_ROWS = 4 * TOP_K
WEIGHT_ROWS = 2 * TOP_K


def _route_tile(logits):
    ne, nt = logits.shape
    eid = lax.broadcasted_iota(I32, (ne, nt), 0).astype(F32)
    neg = jnp.float32(-jnp.inf)
    lg = logits
    picked = jnp.zeros((ne, nt), F32)
    vals, sels, idxs = [], [], []
    for _ in range(TOP_K):
        m = jnp.max(lg, axis=0, keepdims=True)
        idx = jnp.min(jnp.where(lg == m, eid, float(ne)), axis=0, keepdims=True)
        sel = eid == idx
        lg = jnp.where(sel, neg, lg)
        picked = picked + sel.astype(F32)
        vals.append(m)
        sels.append(sel)
        idxs.append(idx)
    ex = [jnp.exp(v - vals[0]) for v in vals]
    den = ex[0] + ex[1] + ex[2] + ex[3]

    earlier = (lax.broadcasted_iota(I32, (nt, nt), 0) < lax.broadcasted_iota(I32, (nt, nt), 1)).astype(BF16)
    before = _dot(picked.astype(BF16), earlier)
    counts = jnp.sum(picked, axis=1, keepdims=True)
    windows = jnp.floor((counts + (RUN_WINDOW - 1)) * (1.0 / RUN_WINDOW))
    lower = (lax.broadcasted_iota(I32, (ne, ne), 0) > lax.broadcasted_iota(I32, (ne, ne), 1)).astype(BF16)
    run_start = _dot(lower, jnp.broadcast_to(windows, (ne, LANES)).astype(BF16))[:, 0:1] * float(RUN_WINDOW)

    zero_row = jnp.zeros((1, nt), F32)
    ranks = [jnp.sum(jnp.where(s, before, 0.0), axis=0, keepdims=True) for s in sels]
    places = [jnp.sum(jnp.where(s, run_start + before, 0.0), axis=0, keepdims=True) for s in sels]
    packed = jnp.concatenate(places + idxs + ranks + [zero_row] * (ROUTE_ROWS - 3 * TOP_K), axis=0)
    wts = jnp.concatenate([e / den for e in ex] + [zero_row] * (WEIGHT_ROWS - TOP_K), axis=0)
    return packed.astype(I32), wts, counts


def _stage_a_kernel(n_seq, seq_rows, carry_halo, pos0, *refs):
    if carry_halo:
        (x_ref, nm_ref, win_ref, lng_ref, lnb_ref, wsp_ref, bsp_ref, wpool_ref, pscale_ref,
         wa_ref, wb_ref, wout_ref, nmoe_ref, wr_ref, br_ref,
         h1_ref, xn2_ref, route_ref, rw_ref, cnt_ref, pool_ref, vn_ref, xc_ref, a_ref) = refs
        cache_ref = None
    else:
        (x_ref, cache_ref, nm_ref, win_ref, lng_ref, lnb_ref, wsp_ref, bsp_ref, wpool_ref,
         pscale_ref, wa_ref, wb_ref, wout_ref, nmoe_ref, wr_ref, br_ref,
         h1_ref, xn2_ref, route_ref, rw_ref, cnt_ref, pool_ref, vn_ref, xc_ref, a_ref) = refs
    d = x_ref.shape[-1]
    head_dim = d // GMLP_HEADS
    group_dim = d // len(POOL_WINDOWS)
    blk = min(GMLP_BLOCK, seq_rows)
    j = pl.program_id(1) if carry_halo else 0

    x = x_ref[...]
    n = _rms(x, nm_ref[...]).astype(BF16)

    def proj(seg):
        return _dot(n, win_ref[:, seg * d:(seg + 1) * d])

    v = jax.nn.gelu(proj(1))
    mu = jnp.mean(v, axis=-1, keepdims=True)
    vc = v - mu
    vn_ref[...] = vc * lax.rsqrt(jnp.mean(vc * vc, axis=-1, keepdims=True) + EPS) * lng_ref[...] + lnb_ref[...]
    u = jax.nn.gelu(proj(0))

    ri = lax.broadcasted_iota(I32, (GMLP_BLOCK, GMLP_BLOCK), 0) // CHUNK
    ci = lax.broadcasted_iota(I32, (GMLP_BLOCK, GMLP_BLOCK), 1) // CHUNK
    causal = (ri >= ci).astype(F32)
    for h in range(GMLP_HEADS):
        wm = (wsp_ref[h] * causal)[:blk, :blk].astype(BF16)
        bias = bsp_ref[:blk, h:h + 1]
        cols = slice(h * head_dim, (h + 1) * head_dim)
        for r0 in range(0, n_seq * seq_rows, blk):
            rows = slice(r0, r0 + blk)
            sg = _dot(wm, vn_ref[rows, cols].astype(BF16)) + bias
            a_ref[rows, cols] = (u[rows, cols] * sg).astype(BF16)
    y_a = _dot(a_ref[...], wa_ref[...])

    pb = proj(2)
    if carry_halo:
        @pl.when(j == 0)
        def _():
            xc_ref[0, 0:HALO, :] = jnp.zeros((HALO, d), F32)
    else:
        for s in range(n_seq):
            xc_ref[s, 0:HALO, :] = jnp.zeros((HALO, d), F32)
            xc_ref[s, HALO - POOL_STATE:HALO, :] = cache_ref[s]
    for s in range(n_seq):
        xc_ref[s, HALO:HALO + seq_rows, :] = pb[s * seq_rows:(s + 1) * seq_rows]

    if carry_halo:
        pos = pos0 + j * seq_rows + lax.broadcasted_iota(I32, (seq_rows, 1), 0)
    z_groups = []
    for g, w in enumerate(POOL_WINDOWS):
        cols = slice(g * group_dim, (g + 1) * group_dim)
        if carry_halo:
            cnt = jnp.minimum(pos + 1, w).astype(F32)
        else:
            cnt = float(min(pos0 + 1, w))
        parts = []
        for s in range(n_seq):
            acc = xc_ref[s, HALO:HALO + seq_rows, cols]
            for k in range(1, w):
                acc = acc + xc_ref[s, HALO - k:HALO - k + seq_rows, cols]
            parts.append(acc / cnt - xc_ref[s, HALO:HALO + seq_rows, cols])
        zg = parts[0] if n_seq == 1 else jnp.concatenate(parts, axis=0)
        z_groups.append(_dot(zg.astype(BF16), wpool_ref[g]) * pscale_ref[:, cols])
    z = jnp.concatenate(z_groups, axis=1)
    y_b = _dot(z.astype(BF16), wb_ref[...])

    for s in range(n_seq):
        tail = xc_ref[s, HALO + seq_rows - POOL_STATE:HALO + seq_rows, :]
        if carry_halo:
            @pl.when(j == pl.num_programs(1) - 1)
            def _():
                pool_ref[...] = tail
        else:
            pool_ref[s] = tail
    if carry_halo:
        xc_ref[0, 0:HALO, :] = xc_ref[0, seq_rows:seq_rows + HALO, :]

    g_a = jax.nn.sigmoid(proj(3))
    g_b = jax.nn.sigmoid(proj(4))
    merged = (g_a * y_a + g_b * y_b).astype(BF16)
    h1 = x + _dot(merged, wout_ref[...])
    h1_ref[...] = h1

    xn2 = _rms(h1, nmoe_ref[...]).astype(BF16)
    xn2_ref[...] = xn2.astype(F32).reshape(xn2_ref.shape)
    for t in range(n_seq * seq_rows // TOKEN_TILE):
        rows = slice(t * TOKEN_TILE, (t + 1) * TOKEN_TILE)
        logits = lax.dot_general(wr_ref[...], xn2[rows], (((1,), (1,)), ((), ())),
                                 preferred_element_type=F32) + br_ref[...]
        packed, wts, counts = _route_tile(logits)
        route_ref[t] = packed
        rw_ref[t] = wts
        cnt_ref[t] = jnp.broadcast_to(counts, (N_EXPERTS, LANES))


def _const_spec(shape, grid_rank):
    zeros = (0,) * len(shape)
    if grid_rank == 2:
        return pl.BlockSpec(shape, lambda b, j: zeros, pipeline_mode=pl.Buffered(1))
    return pl.BlockSpec(shape, lambda i: zeros, pipeline_mode=pl.Buffered(1))


def _stage_a_outputs(n_rows, d):
    return (jax.ShapeDtypeStruct((n_rows, d), F32),
            jax.ShapeDtypeStruct((n_rows, SUBLANES, d // SUBLANES), F32),
            jax.ShapeDtypeStruct((n_rows // TOKEN_TILE, ROUTE_ROWS, TOKEN_TILE), I32),
            jax.ShapeDtypeStruct((n_rows // TOKEN_TILE, WEIGHT_ROWS, TOKEN_TILE), F32),
            jax.ShapeDtypeStruct((n_rows // TOKEN_TILE, N_EXPERTS, LANES), F32))


def _stage_a_prompt(x_prompt, weights):
    nb, seq, d = x_prompt.shape
    tiles_per_seq = seq // TOKEN_TILE
    tok_map = lambda b, j: (b * tiles_per_seq + j, 0)
    tok_map3 = lambda b, j: (b * tiles_per_seq + j, 0, 0)
    return pl.pallas_call(
        functools.partial(_stage_a_kernel, 1, TOKEN_TILE, True, 0),
        grid=(nb, tiles_per_seq),
        in_specs=[pl.BlockSpec((None, TOKEN_TILE, d), lambda b, j: (b, j, 0))]
        + [_const_spec(w.shape, 2) for w in weights],
        out_specs=(
            pl.BlockSpec((TOKEN_TILE, d), tok_map),
            pl.BlockSpec((TOKEN_TILE, SUBLANES, d // SUBLANES), tok_map3),
            pl.BlockSpec((1, ROUTE_ROWS, TOKEN_TILE), tok_map3),
            pl.BlockSpec((1, WEIGHT_ROWS, TOKEN_TILE), tok_map3),
            pl.BlockSpec((1, N_EXPERTS, LANES), tok_map3),
            pl.BlockSpec((None, POOL_STATE, d), lambda b, j: (b, 0, 0)),
        ),
        out_shape=_stage_a_outputs(nb * seq, d) + (jax.ShapeDtypeStruct((nb, POOL_STATE, d), F32),),
        scratch_shapes=[
            pltpu.VMEM((TOKEN_TILE, d), F32),
            pltpu.VMEM((1, HALO + TOKEN_TILE, d), F32),
            pltpu.VMEM((TOKEN_TILE, d), BF16),
        ],
        compiler_params=pltpu.CompilerParams(
            dimension_semantics=("arbitrary", "arbitrary"), vmem_limit_bytes=VMEM_LIMIT),
        name="stage_a_prompt",
    )(x_prompt, *weights)


def _stage_a_sample(x_sample, cache, past_len, weights):
    sb, sseq, d = x_sample.shape
    n_s = sb * sseq
    whole2 = lambda i: (0, 0)
    whole3 = lambda i: (0, 0, 0)
    return pl.pallas_call(
        functools.partial(_stage_a_kernel, sb, sseq, False, past_len),
        grid=(1,),
        in_specs=[pl.BlockSpec((n_s, d), whole2), pl.BlockSpec((sb, POOL_STATE, d), whole3)]
        + [_const_spec(w.shape, 1) for w in weights],
        out_specs=(
            pl.BlockSpec((n_s, d), whole2),
            pl.BlockSpec((n_s, SUBLANES, d // SUBLANES), whole3),
            pl.BlockSpec((n_s // TOKEN_TILE, ROUTE_ROWS, TOKEN_TILE), whole3),
            pl.BlockSpec((n_s // TOKEN_TILE, WEIGHT_ROWS, TOKEN_TILE), whole3),
            pl.BlockSpec((n_s // TOKEN_TILE, N_EXPERTS, LANES), whole3),
            pl.BlockSpec((sb, POOL_STATE, d), whole3),
            pl.BlockSpec((n_s, d), whole2),
        ),
        out_shape=_stage_a_outputs(n_s, d) + (jax.ShapeDtypeStruct((sb, POOL_STATE, d), F32),
                                              jax.ShapeDtypeStruct((n_s, d), F32)),
        scratch_shapes=[
            pltpu.VMEM((sb, HALO + sseq, d), F32),
            pltpu.VMEM((n_s, d), BF16),
        ],
        compiler_params=pltpu.CompilerParams(
            dimension_semantics=("arbitrary",), vmem_limit_bytes=VMEM_LIMIT),
        name="stage_a_sample",
    )(x_sample.reshape(n_s, d), cache, *weights)


def _for_tile_rows(fn):
    def body(g, c):
        for rr in range(SUBLANES):
            for k in range(TOP_K):
                fn(g * SUBLANES + rr, k)
        return c

    lax.fori_loop(0, TOKEN_TILE // SUBLANES, body, 0)


def _loop(lo, hi, fn):
    def body(i, c):
        fn(i)
        return c

    lax.fori_loop(lo, hi, body, 0)


def _dispatch_kernel(n_prompt_tiles, lpos_ref, run_dst_ref, run_src_ref, run_win_ref, pad_lo_ref, pad_hi_ref,
                     nt_ref, route_ref, xp_ref, xsm_ref, xs_hbm, pos_ref, lbuf, zbuf, sems):
    i = pl.program_id(0)
    last = pl.num_programs(0) - 1
    slot = i % 2

    routed = route_ref[0]
    expert = routed[TOP_K:2 * TOP_K]
    group_row = jnp.zeros(expert.shape, I32)
    for e in range(N_EXPERTS):
        group_row = jnp.where(expert == e, run_dst_ref[i * N_EXPERTS + e], group_row)
    pos_ref[0] = group_row + routed[2 * TOP_K:3 * TOP_K]

    @pl.when(i == 0)
    def _():
        lbuf[...] = jnp.zeros(lbuf.shape, F32)
        zbuf[...] = jnp.zeros(zbuf.shape, F32)
        pad_sem = sems.at[0]

        def pad_group(e):
            lo, hi = pad_lo_ref[e], pad_hi_ref[e]
            n_big = (hi - lo) // ZERO_ROWS
            mid = lo + n_big * ZERO_ROWS
            n_mid = (hi - mid) // SUBLANES
            fine = mid + n_mid * SUBLANES
            big = lambda c: pltpu.make_async_copy(zbuf, xs_hbm.at[pl.ds(lo + c * ZERO_ROWS, ZERO_ROWS)], pad_sem)
            med = lambda c: pltpu.make_async_copy(zbuf.at[pl.ds(0, SUBLANES)],
                                                  xs_hbm.at[pl.ds(mid + c * SUBLANES, SUBLANES)], pad_sem)
            one = lambda r: pltpu.make_async_copy(zbuf.at[0], xs_hbm.at[r], pad_sem)
            _loop(0, n_big, lambda c: big(c).start())
            _loop(0, n_mid, lambda c: med(c).start())
            _loop(fine, hi, lambda r: one(r).start())
            _loop(0, n_big, lambda c: big(c).wait())
            _loop(0, n_mid, lambda c: med(c).wait())
            _loop(fine, hi, lambda r: one(r).wait())

        _loop(0, N_EXPERTS, pad_group)

        n_chunks = ROW_TILE // ZERO_ROWS
        chunk = lambda c: pltpu.make_async_copy(zbuf, xs_hbm.at[pl.ds(c * ZERO_ROWS, ZERO_ROWS)], pad_sem)
        _loop(nt_ref[0] * n_chunks, (xs_hbm.shape[0] // ROW_TILE) * n_chunks, lambda c: chunk(c).start())
        _loop(nt_ref[0] * n_chunks, (xs_hbm.shape[0] // ROW_TILE) * n_chunks, lambda c: chunk(c).wait())

    def place(src_ref):
        def body(r, k):
            lbuf[slot, lpos_ref[k * TOKEN_TILE + r]] = src_ref[r]
        _for_tile_rows(body)

    @pl.when(i < n_prompt_tiles)
    def _():
        place(xp_ref)

    @pl.when(i >= n_prompt_tiles)
    def _():
        place(xsm_ref)

    def run_copies(tile, s, fn):
        def per_pair(e2):
            for queue in range(2):
                t = tile * N_EXPERTS + e2 * 2 + queue
                src0, dst0 = run_src_ref[t], run_dst_ref[t]

                def per_window(w, src0=src0, dst0=dst0, queue=queue):
                    fn(pltpu.make_async_copy(lbuf.at[s, pl.ds(src0 + w * RUN_WINDOW, RUN_WINDOW)],
                                             xs_hbm.at[pl.ds(dst0 + w * RUN_WINDOW, RUN_WINDOW)], sems.at[s]), queue)

                _loop(0, run_win_ref[t], per_window)

        _loop(0, N_EXPERTS // 2, per_pair)

    @pl.when(i > 0)
    def _():
        run_copies(i - 1, 1 - slot, lambda c, q: c.wait())

    run_copies(i, slot, lambda c, q: c.start(priority=q))

    @pl.when(i == last)
    def _():
        run_copies(i, slot, lambda c, q: c.wait())


def _dispatch(lpos_flat, route, run_dst, run_src, run_win, pad_lo, pad_hi, num_tiles, xn2_p, xn2_s, n_rows):
    row_shape = xn2_p.shape[1:]
    npt = xn2_p.shape[0] // TOKEN_TILE
    nst = xn2_s.shape[0] // TOKEN_TILE
    smem_all = lambda a: pl.BlockSpec(a.shape, lambda i: (0,), memory_space=pltpu.SMEM)
    return pl.pallas_call(
        functools.partial(_dispatch_kernel, npt),
        grid=(npt + nst,),
        in_specs=[pl.BlockSpec((TOKEN_TILE * TOP_K,), lambda i: (i,), memory_space=pltpu.SMEM),
                  smem_all(run_dst), smem_all(run_src), smem_all(run_win),
                  smem_all(pad_lo), smem_all(pad_hi), smem_all(num_tiles),
                  pl.BlockSpec((1, ROUTE_ROWS, TOKEN_TILE), lambda i: (i, 0, 0)),
                  pl.BlockSpec((TOKEN_TILE,) + row_shape, lambda i: (jnp.minimum(i, npt - 1), 0, 0)),
                  pl.BlockSpec((TOKEN_TILE,) + row_shape, lambda i: (jnp.maximum(i - npt, 0), 0, 0))],
        out_specs=(pl.BlockSpec(memory_space=pl.ANY),
                   pl.BlockSpec((1, TOP_K, TOKEN_TILE), lambda i: (i, 0, 0))),
        out_shape=(jax.ShapeDtypeStruct((n_rows,) + row_shape, F32),
                   jax.ShapeDtypeStruct((npt + nst, TOP_K, TOKEN_TILE), I32)),
        scratch_shapes=[pltpu.VMEM((2, LOCAL_ROWS) + row_shape, F32),
                        pltpu.VMEM((ZERO_ROWS,) + row_shape, F32),
                        pltpu.SemaphoreType.DMA((2,))],
        compiler_params=pltpu.CompilerParams(
            dimension_semantics=("arbitrary",), vmem_limit_bytes=VMEM_LIMIT),
        name="dispatch",
    )(lpos_flat, run_dst, run_src, run_win, pad_lo, pad_hi, num_tiles, route, xn2_p, xn2_s)


def _expert_kernel(te_ref, nt_ref, xs_ref, wup_ref, bup_ref, wdn_ref, bdn_ref, ys_ref, wup_b, wdn_b):
    d_ff = wdn_ref.shape[0]
    i = pl.program_id(0)
    used = i < nt_ref[0]

    @pl.when(jnp.logical_or(i == 0, te_ref[i] != te_ref[jnp.maximum(i - 1, 0)]))
    def _():
        wup_b[...] = wup_ref[...].astype(BF16)
        wdn_b[...] = wdn_ref[...].astype(BF16)

    @pl.when(used)
    def _():
        x = xs_ref[...].reshape(ROW_TILE, -1).astype(BF16)
        hid = _dot(x, wup_b[...]) + bup_ref[...]
        glu = jnp.minimum(hid[:, :d_ff], SWIGLU_LIMIT)
        lin = jnp.clip(hid[:, d_ff:], -SWIGLU_LIMIT, SWIGLU_LIMIT)
        act = glu * jax.nn.sigmoid(SWIGLU_ALPHA * glu) * (lin + 1.0)
        y = _dot(act.astype(BF16), wdn_b[...]) + bdn_ref[...]
        ys_ref[...] = y.reshape(ys_ref.shape)

    @pl.when(jnp.logical_not(used))
    def _():
        ys_ref[...] = jnp.zeros(ys_ref.shape, F32)


def _experts(tile_expert, num_tiles, xs, w_up, b_up, w_down, b_down):
    n_rows = xs.shape[0]
    row_shape = xs.shape[1:]
    n_e, d, d_up = w_up.shape
    d_ff = w_down.shape[1]
    exp_map = lambda i, te, nt: (te[i], 0, 0)
    grid_spec = pltpu.PrefetchScalarGridSpec(
        num_scalar_prefetch=2,
        grid=(n_rows // ROW_TILE,),
        in_specs=[
            pl.BlockSpec((ROW_TILE,) + row_shape, lambda i, te, nt: (jnp.minimum(i, nt[0] - 1), 0, 0)),
            pl.BlockSpec((None, d, d_up), exp_map),
            pl.BlockSpec((None, 1, d_up), exp_map),
            pl.BlockSpec((None, d_ff, d), exp_map),
            pl.BlockSpec((None, 1, d), exp_map),
        ],
        out_specs=pl.BlockSpec((ROW_TILE,) + row_shape, lambda i, te, nt: (i, 0, 0)),
        scratch_shapes=[pltpu.VMEM((d, d_up), BF16), pltpu.VMEM((d_ff, d), BF16)],
    )
    return pl.pallas_call(
        _expert_kernel,
        grid_spec=grid_spec,
        out_shape=jax.ShapeDtypeStruct((n_rows,) + row_shape, F32),
        compiler_params=pltpu.CompilerParams(
            dimension_semantics=("arbitrary",), vmem_limit_bytes=VMEM_LIMIT),
        name="experts",
    )(tile_expert, num_tiles, xs, w_up, b_up.reshape(n_e, 1, d_up), w_down, b_down.reshape(n_e, 1, d))


def _combine_kernel(n_prompt_tiles, pos_ref, pos_next_ref, ys_hbm, w_ref, h1p_ref, h1s_ref, pp_ref, ps_ref,
                    nple_ref, wg_ref, wp_ref, nfin_ref, yp_ref, ysm_ref, gbuf, sems):
    i = pl.program_id(0)
    is_prompt = i < n_prompt_tiles
    slot = i % 2

    def gather(p_ref, s):
        return lambda r, k: pltpu.make_async_copy(ys_hbm.at[p_ref[k * TOKEN_TILE + r]], gbuf.at[s, k, r],
                                                  sems.at[s])

    @pl.when(i == 0)
    def _():
        copy = gather(pos_ref, 0)
        _for_tile_rows(lambda r, k: copy(r, k).start(priority=k % 2))

    @pl.when(i + 1 < pl.num_programs(0))
    def _():
        copy = gather(pos_next_ref, 1 - slot)
        _for_tile_rows(lambda r, k: copy(r, k).start(priority=k % 2))

    copy = gather(pos_ref, slot)
    _for_tile_rows(lambda r, k: copy(r, k).wait())

    d = h1p_ref.shape[-1]
    wts = jnp.transpose(w_ref[0])
    moe = wts[:, 0:1] * gbuf[slot, 0].reshape(TOKEN_TILE, d)
    for k in range(1, TOP_K):
        moe = moe + wts[:, k:k + 1] * gbuf[slot, k].reshape(TOKEN_TILE, d)
    h2 = jnp.where(is_prompt, h1p_ref[...], h1s_ref[...]) + moe
    p = jnp.where(is_prompt, pp_ref[...], ps_ref[...])
    n3 = _rms(h2, nple_ref[...]).astype(BF16)
    gate = jax.nn.sigmoid(_dot(n3, wg_ref[...]))
    h3 = h2 + gate * _dot(p.astype(BF16), wp_ref[...])
    y = _rms(h3, nfin_ref[...])

    @pl.when(is_prompt)
    def _():
        yp_ref[...] = y

    @pl.when(jnp.logical_not(is_prompt))
    def _():
        ysm_ref[...] = y


def _combine(pos_flat, ys, topw, h1_p, h1_s, p_p, p_s, nple, wg, wp, nfin):
    n_p, d = h1_p.shape
    n_s = h1_s.shape[0]
    ple = p_p.shape[1]
    npt = n_p // TOKEN_TILE
    nst = n_s // TOKEN_TILE
    p_tok = lambda i: (jnp.minimum(i, npt - 1), 0)
    s_tok = lambda i: (jnp.maximum(i - npt, 0), 0)
    const = lambda i: (0, 0)
    last = npt + nst - 1
    this_tile = pl.BlockSpec((TOKEN_TILE * TOP_K,), lambda i: (i,), memory_space=pltpu.SMEM)
    next_tile = pl.BlockSpec((TOKEN_TILE * TOP_K,), lambda i: (jnp.minimum(i + 1, last),), memory_space=pltpu.SMEM)
    return pl.pallas_call(
        functools.partial(_combine_kernel, npt),
        grid=(npt + nst,),
        in_specs=[
            this_tile, next_tile,
            pl.BlockSpec(memory_space=pl.ANY),
            pl.BlockSpec((1, WEIGHT_ROWS, TOKEN_TILE), lambda i: (i, 0, 0)),
            pl.BlockSpec((TOKEN_TILE, d), p_tok),
            pl.BlockSpec((TOKEN_TILE, d), s_tok),
            pl.BlockSpec((TOKEN_TILE, ple), p_tok),
            pl.BlockSpec((TOKEN_TILE, ple), s_tok),
            pl.BlockSpec((1, d), const),
            pl.BlockSpec((d, d), const),
            pl.BlockSpec((ple, d), const),
            pl.BlockSpec((1, d), const),
        ],
        out_specs=(pl.BlockSpec((TOKEN_TILE, d), p_tok), pl.BlockSpec((TOKEN_TILE, d), s_tok)),
        out_shape=(jax.ShapeDtypeStruct((n_p, d), F32), jax.ShapeDtypeStruct((n_s, d), F32)),
        scratch_shapes=[pltpu.VMEM((2, TOP_K, TOKEN_TILE) + ys.shape[1:], F32), pltpu.SemaphoreType.DMA((2,))],
        compiler_params=pltpu.CompilerParams(
            dimension_semantics=("arbitrary",), vmem_limit_bytes=VMEM_LIMIT),
        name="combine",
    )(pos_flat, pos_flat, ys, topw, h1_p, h1_s, p_p, p_s, nple, wg, wp, nfin)


def kernel(x_prompt, x_sample, cache_pool, p_prompt, p_sample, norm_mix, w_in, ln_v_g, ln_v_b, w_spatial, b_spatial, w_pool_group, pool_scale, w_branch_a, w_branch_b, w_out, norm_moe, w_router, b_router, w_up, b_up, w_down, b_down, norm_ple, w_ple_gate, w_ple_proj, norm_final):
    assert norm_mix.shape[0] == 1, "single-layer stack"
    nb, seq, d = x_prompt.shape
    sb, sseq, _ = x_sample.shape
    n_prompt = nb * seq
    n_sample = sb * sseq
    n_tok = n_prompt + n_sample
    past_len = 4096
    row = lambda v: v.reshape(1, -1)

    wr = jnp.transpose(w_router[0]).astype(BF16)
    br = b_router[0].reshape(N_EXPERTS, 1)
    weights_a = (row(norm_mix[0]), w_in[0].astype(BF16), row(ln_v_g[0]), row(ln_v_b[0]), w_spatial[0],
                 jnp.transpose(b_spatial[0]), w_pool_group[0].astype(BF16), row(pool_scale[0]),
                 w_branch_a[0].astype(BF16), w_branch_b[0].astype(BF16), w_out[0].astype(BF16),
                 row(norm_moe[0]), wr, br)
    h1_p, xn2_p, route_p, rw_p, cnt_p, pool_p = _stage_a_prompt(x_prompt, weights_a)
    h1_s, xn2_s, route_s, rw_s, cnt_s, pool_s, vn_s = _stage_a_sample(x_sample, cache_pool[0], past_len, weights_a)

    n_tab = jnp.concatenate([cnt_p[:, :, 0], cnt_s[:, :, 0]], axis=0).astype(I32)
    ahead = jnp.cumsum(n_tab, axis=0) - n_tab
    total = jnp.sum(n_tab, axis=0)
    tiles_e = (total + (RUN_WINDOW - 1) + (ROW_TILE - 1)) // ROW_TILE
    ends = jnp.cumsum(tiles_e)
    first = (ends - tiles_e) * ROW_TILE
    num_tiles = ends[-1:]
    max_tiles = (n_tok * TOP_K + N_EXPERTS * (RUN_WINDOW - 1)) // ROW_TILE + N_EXPERTS
    tile_ids = jnp.minimum(jnp.arange(max_tiles, dtype=I32), num_tiles[0] - 1)
    tile_expert = jnp.sum((tile_ids[:, None] >= ends[None, :]).astype(I32), axis=1)
    run_dst = (first[None, :] + ahead).reshape(-1)
    run_win = (n_tab + (RUN_WINDOW - 1)) // RUN_WINDOW
    run_src = ((jnp.cumsum(run_win, axis=1) - run_win) * RUN_WINDOW).reshape(-1)
    pad_lo = first + total
    pad_hi = ends * ROW_TILE

    route = jnp.concatenate([route_p, route_s], axis=0)
    topw = jnp.concatenate([rw_p, rw_s], axis=0)
    lpos_flat = route[:, :TOP_K].reshape(-1)

    xs, pos = _dispatch(lpos_flat, route, run_dst, run_src, run_win.reshape(-1), pad_lo, pad_hi, num_tiles,
                        xn2_p, xn2_s, max_tiles * ROW_TILE)
    ys = _experts(tile_expert, num_tiles, xs, w_up[0], b_up[0], w_down[0], b_down[0])

    y_p, y_s = _combine(pos.reshape(-1), ys, topw, h1_p, h1_s, p_prompt[0].reshape(n_prompt, -1),
                        p_sample[0].reshape(n_sample, -1), row(norm_ple[0]), w_ple_gate[0].astype(BF16),
                        w_ple_proj[0].astype(BF16), row(norm_final))

    return (y_p.reshape(nb, seq, d), y_s.reshape(sb, sseq, d), pool_p[None], pool_s[None],
            vn_s.reshape(1, sb, sseq, d))
```

```python
import functools

import jax
import jax.numpy as jnp
from jax import lax
from jax.experimental import pallas as pl
from jax.experimental.pallas import tpu as pltpu

F32 = jnp.float32
BF16 = jnp.bfloat16
I32 = jnp.int32

EPS = 1e-6
CHUNK = 64
GMLP_BLOCK = 128
GMLP_HEADS = 4
POOL_WINDOWS = (2, 4, 8, 16)
POOL_STATE = 15
SUBLANES = 8
HALO = SUBLANES * len(POOL_WINDOWS)
N_EXPERTS = 32
TOP_K = 4
SWIGLU_LIMIT = 7.0
SWIGLU_ALPHA = 1.702
LANES = 128

STAGE_TILE = 512
TOKEN_TILE = 256
ROW_TILE = 512
RUN_WINDOW = 16
LOCAL_ROWS = N_EXPERTS * RUN_WINDOW + TOKEN_TILE * TOP_K
ZERO_ROWS = 64
VMEM_LIMIT = 56 * 1024 * 1024


def _rms(xf, g):
    return xf * lax.rsqrt(jnp.mean(xf * xf, axis=-1, keepdims=True) + EPS) * g


def _dot(a, b):
    return jnp.dot(a, b, preferred_element_type=F32)


ROUTE_ROWS = 4 * TOP_K
WEIGHT_ROWS = 2 * TOP_K


def _route_tile(logits):
    ne, nt = logits.shape
    eid = lax.broadcasted_iota(I32, (ne, nt), 0).astype(F32)
    neg = jnp.float32(-jnp.inf)
    lg = logits
    picked = jnp.zeros((ne, nt), F32)
    vals, sels, idxs = [], [], []
    for _ in range(TOP_K):
        m = jnp.max(lg, axis=0, keepdims=True)
        idx = jnp.min(jnp.where(lg == m, eid, float(ne)), axis=0, keepdims=True)
        sel = eid == idx
        lg = jnp.where(sel, neg, lg)
        picked = picked + sel.astype(F32)
        vals.append(m)
        sels.append(sel)
        idxs.append(idx)
    ex = [jnp.exp(v - vals[0]) for v in vals]
    den = ex[0] + ex[1] + ex[2] + ex[3]

    earlier = (lax.broadcasted_iota(I32, (nt, nt), 0) < lax.broadcasted_iota(I32, (nt, nt), 1)).astype(BF16)
    before = _dot(picked.astype(BF16), earlier)
    counts = jnp.sum(picked, axis=1, keepdims=True)
    windows = jnp.floor((counts + (RUN_WINDOW - 1)) * (1.0 / RUN_WINDOW))
    lower = (lax.broadcasted_iota(I32, (ne, ne), 0) > lax.broadcasted_iota(I32, (ne, ne), 1)).astype(BF16)
    run_start = _dot(lower, jnp.broadcast_to(windows, (ne, LANES)).astype(BF16))[:, 0:1] * float(RUN_WINDOW)

    zero_row = jnp.zeros((1, nt), F32)
    ranks = [jnp.sum(jnp.where(s, before, 0.0), axis=0, keepdims=True) for s in sels]
    places = [jnp.sum(jnp.where(s, run_start + before, 0.0), axis=0, keepdims=True) for s in sels]
    packed = jnp.concatenate(places + idxs + ranks + [zero_row] * (ROUTE_ROWS - 3 * TOP_K), axis=0)
    wts = jnp.concatenate([e / den for e in ex] + [zero_row] * (WEIGHT_ROWS - TOP_K), axis=0)
    return packed.astype(I32), wts, counts


def _stage_a_kernel(n_seq, seq_rows, carry_halo, pos0, *refs):
    if carry_halo:
        (x_ref, nm_ref, win_ref, lng_ref, lnb_ref, wsp_ref, bsp_ref, wpool_ref, pscale_ref,
         wa_ref, wb_ref, wout_ref, nmoe_ref, wr_ref, br_ref,
         h1_ref, xn2_ref, route_ref, rw_ref, cnt_ref, pool_ref, vn_ref, xc_ref, lv_ref, a_ref) = refs
        cache_ref = None
    else:
        (x_ref, cache_ref, nm_ref, win_ref, lng_ref, lnb_ref, wsp_ref, bsp_ref, wpool_ref,
         pscale_ref, wa_ref, wb_ref, wout_ref, nmoe_ref, wr_ref, br_ref,
         h1_ref, xn2_ref, route_ref, rw_ref, cnt_ref, pool_ref, vn_ref, xc_ref, lv_ref, a_ref) = refs
    d = x_ref.shape[-1]
    head_dim = d // GMLP_HEADS
    group_dim = d // len(POOL_WINDOWS)
    blk = min(GMLP_BLOCK, seq_rows)
    j = pl.program_id(1) if carry_halo else 0

    x = x_ref[...]
    n = _rms(x, nm_ref[...]).astype(BF16)

    def proj(seg):
        return _dot(n, win_ref[:, seg * d:(seg + 1) * d])

    v = jax.nn.gelu(proj(1))
    mu = jnp.mean(v, axis=-1, keepdims=True)
    vc = v - mu
    vn_ref[...] = vc * lax.rsqrt(jnp.mean(vc * vc, axis=-1, keepdims=True) + EPS) * lng_ref[...] + lnb_ref[...]
    u = jax.nn.gelu(proj(0))

    ri = lax.broadcasted_iota(I32, (GMLP_BLOCK, GMLP_BLOCK), 0) // CHUNK
    ci = lax.broadcasted_iota(I32, (GMLP_BLOCK, GMLP_BLOCK), 1) // CHUNK
    causal = (ri >= ci).astype(F32)
    for h in range(GMLP_HEADS):
        wm = (wsp_ref[h] * causal)[:blk, :blk].astype(BF16)
        bias = bsp_ref[:blk, h:h + 1]
        cols = slice(h * head_dim, (h + 1) * head_dim)
        for r0 in range(0, n_seq * seq_rows, blk):
            rows = slice(r0, r0 + blk)
            sg = _dot(wm, vn_ref[rows, cols].astype(BF16)) + bias
            a_ref[rows, cols] = (u[rows, cols] * sg).astype(BF16)
    y_a = _dot(a_ref[...], wa_ref[...])

    pb = proj(2)
    if carry_halo:
        @pl.when(j == 0)
        def _():
            xc_ref[0, 0:HALO, :] = jnp.zeros((HALO, d), F32)
    else:
        for s in range(n_seq):
            xc_ref[s, 0:HALO, :] = jnp.zeros((HALO, d), F32)
            xc_ref[s, HALO - POOL_STATE:HALO, :] = cache_ref[s]
    for s in range(n_seq):
        xc_ref[s, HALO:HALO + seq_rows, :] = pb[s * seq_rows:(s + 1) * seq_rows]

    end = HALO + seq_rows
    window_sums = [[] for _ in POOL_WINDOWS]
    for s in range(n_seq):
        src, shift = xc_ref, 1
        for lvl in range(len(POOL_WINDOWS)):
            lo = SUBLANES * (lvl + 1)
            cols = slice(lvl * group_dim, d)
            summed = src[s, lo:end, cols] + src[s, lo - shift:end - shift, cols]
            window_sums[lvl].append(summed[HALO - lo:, :group_dim])
            if lvl + 1 < len(POOL_WINDOWS):
                dst = lv_ref.at[lvl % 2]
                dst[s, lo:end, (lvl + 1) * group_dim:d] = summed[:, group_dim:]
                src, shift = dst, 2 * shift

    if carry_halo:
        pos = pos0 + j * seq_rows + lax.broadcasted_iota(I32, (seq_rows, 1), 0)
    z_groups = []
    for g, w in enumerate(POOL_WINDOWS):
        cols = slice(g * group_dim, (g + 1) * group_dim)
        if carry_halo:
            cnt = jnp.minimum(pos + 1, w).astype(F32)
        else:
            cnt = float(min(pos0 + 1, w))
        parts = [window_sums[g][s] / cnt - xc_ref[s, HALO:end, cols] for s in range(n_seq)]
        zg = parts[0] if n_seq == 1 else jnp.concatenate(parts, axis=0)
        z_groups.append(_dot(zg.astype(BF16), wpool_ref[g]) * pscale_ref[:, cols])
    z = jnp.concatenate(z_groups, axis=1)
    y_b = _dot(z.astype(BF16), wb_ref[...])

    for s in range(n_seq):
        tail = xc_ref[s, HALO + seq_rows - POOL_STATE:HALO + seq_rows, :]
        if carry_halo:
            @pl.when(j == pl.num_programs(1) - 1)
            def _():
                pool_ref[...] = tail
        else:
            pool_ref[s] = tail
    if carry_halo:
        xc_ref[0, 0:HALO, :] = xc_ref[0, seq_rows:seq_rows + HALO, :]

    g_a = jax.nn.sigmoid(proj(3))
    g_b = jax.nn.sigmoid(proj(4))
    merged = (g_a * y_a + g_b * y_b).astype(BF16)
    h1 = x + _dot(merged, wout_ref[...])
    h1_ref[...] = h1

    xn2 = _rms(h1, nmoe_ref[...]).astype(BF16)
    xn2_ref[...] = xn2.astype(F32).reshape(xn2_ref.shape)
    for t in range(n_seq * seq_rows // TOKEN_TILE):
        rows = slice(t * TOKEN_TILE, (t + 1) * TOKEN_TILE)
        logits = lax.dot_general(wr_ref[...], xn2[rows], (((1,), (1,)), ((), ())),
                                 preferred_element_type=F32) + br_ref[...]
        packed, wts, counts = _route_tile(logits)
        route_ref[t] = packed
        rw_ref[t] = wts
        cnt_ref[t] = jnp.broadcast_to(counts, (N_EXPERTS, LANES))


def _const_spec(shape, grid_rank):
    zeros = (0,) * len(shape)
    if grid_rank == 2:
        return pl.BlockSpec(shape, lambda b, j: zeros, pipeline_mode=pl.Buffered(1))
    return pl.BlockSpec(shape, lambda i: zeros, pipeline_mode=pl.Buffered(1))


def _stage_a_outputs(n_rows, d):
    return (jax.ShapeDtypeStruct((n_rows, d), F32),
            jax.ShapeDtypeStruct((n_rows, SUBLANES, d // SUBLANES), F32),
            jax.ShapeDtypeStruct((n_rows // TOKEN_TILE, ROUTE_ROWS, TOKEN_TILE), I32),
            jax.ShapeDtypeStruct((n_rows // TOKEN_TILE, WEIGHT_ROWS, TOKEN_TILE), F32),
            jax.ShapeDtypeStruct((n_rows // TOKEN_TILE, N_EXPERTS, LANES), F32))


def _stage_a_prompt(x_prompt, weights):
    nb, seq, d = x_prompt.shape
    tiles_per_seq = seq // STAGE_TILE
    sub = STAGE_TILE // TOKEN_TILE
    tok_map = lambda b, j: (b * tiles_per_seq + j, 0)
    tok_map3 = lambda b, j: (b * tiles_per_seq + j, 0, 0)
    return pl.pallas_call(
        functools.partial(_stage_a_kernel, 1, STAGE_TILE, True, 0),
        grid=(nb, tiles_per_seq),
        in_specs=[pl.BlockSpec((None, STAGE_TILE, d), lambda b, j: (b, j, 0))]
        + [_const_spec(w.shape, 2) for w in weights],
        out_specs=(
            pl.BlockSpec((STAGE_TILE, d), tok_map),
            pl.BlockSpec((STAGE_TILE, SUBLANES, d // SUBLANES), tok_map3),
            pl.BlockSpec((sub, ROUTE_ROWS, TOKEN_TILE), tok_map3),
            pl.BlockSpec((sub, WEIGHT_ROWS, TOKEN_TILE), tok_map3),
            pl.BlockSpec((sub, N_EXPERTS, LANES), tok_map3),
            pl.BlockSpec((None, POOL_STATE, d), lambda b, j: (b, 0, 0)),
        ),
        out_shape=_stage_a_outputs(nb * seq, d) + (jax.ShapeDtypeStruct((nb, POOL_STATE, d), F32),),
        scratch_shapes=[
            pltpu.VMEM((STAGE_TILE, d), F32),
            pltpu.VMEM((1, HALO + STAGE_TILE, d), F32),
            pltpu.VMEM((2, 1, HALO + STAGE_TILE, d), F32),
            pltpu.VMEM((STAGE_TILE, d), BF16),
        ],
        compiler_params=pltpu.CompilerParams(
            dimension_semantics=("arbitrary", "arbitrary"), vmem_limit_bytes=VMEM_LIMIT),
        name="stage_a_prompt",
    )(x_prompt, *weights)


def _stage_a_sample(x_sample, cache, past_len, weights):
    sb, sseq, d = x_sample.shape
    n_s = sb * sseq
    whole2 = lambda i: (0, 0)
    whole3 = lambda i: (0, 0, 0)
    return pl.pallas_call(
        functools.partial(_stage_a_kernel, sb, sseq, False, past_len),
        grid=(1,),
        in_specs=[pl.BlockSpec((n_s, d), whole2), pl.BlockSpec((sb, POOL_STATE, d), whole3)]
        + [_const_spec(w.shape, 1) for w in weights],
        out_specs=(
            pl.BlockSpec((n_s, d), whole2),
            pl.BlockSpec((n_s, SUBLANES, d // SUBLANES), whole3),
            pl.BlockSpec((n_s // TOKEN_TILE, ROUTE_ROWS, TOKEN_TILE), whole3),
            pl.BlockSpec((n_s // TOKEN_TILE, WEIGHT_ROWS, TOKEN_TILE), whole3),
            pl.BlockSpec((n_s // TOKEN_TILE, N_EXPERTS, LANES), whole3),
            pl.BlockSpec((sb, POOL_STATE, d), whole3),
            pl.BlockSpec((n_s, d), whole2),
        ),
        out_shape=_stage_a_outputs(n_s, d) + (jax.ShapeDtypeStruct((sb, POOL_STATE, d), F32),
                                              jax.ShapeDtypeStruct((n_s, d), F32)),
        scratch_shapes=[
            pltpu.VMEM((sb, HALO + sseq, d), F32),
            pltpu.VMEM((2, sb, HALO + sseq, d), F32),
            pltpu.VMEM((n_s, d), BF16),
        ],
        compiler_params=pltpu.CompilerParams(
            dimension_semantics=("arbitrary",), vmem_limit_bytes=VMEM_LIMIT),
        name="stage_a_sample",
    )(x_sample.reshape(n_s, d), cache, *weights)


def _for_tile_rows(fn):
    def body(g, c):
        for rr in range(SUBLANES):
            for k in range(TOP_K):
                fn(g * SUBLANES + rr, k)
        return c

    lax.fori_loop(0, TOKEN_TILE // SUBLANES, body, 0)


def _loop(lo, hi, fn):
    def body(i, c):
        fn(i)
        return c

    lax.fori_loop(lo, hi, body, 0)


def _dispatch_kernel(n_prompt_tiles, lpos_ref, run_dst_ref, run_src_ref, run_win_ref, pad_lo_ref, pad_hi_ref,
                     nt_ref, route_ref, xp_ref, xsm_ref, xs_hbm, pos_ref, lbuf, zbuf, sems):
    i = pl.program_id(0)
    last = pl.num_programs(0) - 1
    slot = i % 2

    routed = route_ref[0]
    expert = routed[TOP_K:2 * TOP_K]
    group_row = jnp.zeros(expert.shape, I32)
    for e in range(N_EXPERTS):
        group_row = jnp.where(expert == e, run_dst_ref[i * N_EXPERTS + e], group_row)
    pos_ref[0] = group_row + routed[2 * TOP_K:3 * TOP_K]

    @pl.when(i == 0)
    def _():
        lbuf[...] = jnp.zeros(lbuf.shape, F32)
        zbuf[...] = jnp.zeros(zbuf.shape, F32)
        pad_sem = sems.at[0]

        def pad_group(e):
            lo, hi = pad_lo_ref[e], pad_hi_ref[e]
            n_big = (hi - lo) // ZERO_ROWS
            mid = lo + n_big * ZERO_ROWS
            n_mid = (hi - mid) // SUBLANES
            fine = mid + n_mid * SUBLANES
            big = lambda c: pltpu.make_async_copy(zbuf, xs_hbm.at[pl.ds(lo + c * ZERO_ROWS, ZERO_ROWS)], pad_sem)
            med = lambda c: pltpu.make_async_copy(zbuf.at[pl.ds(0, SUBLANES)],
                                                  xs_hbm.at[pl.ds(mid + c * SUBLANES, SUBLANES)], pad_sem)
            one = lambda r: pltpu.make_async_copy(zbuf.at[0], xs_hbm.at[r], pad_sem)
            _loop(0, n_big, lambda c: big(c).start())
            _loop(0, n_mid, lambda c: med(c).start())
            _loop(fine, hi, lambda r: one(r).start())
            _loop(0, n_big, lambda c: big(c).wait())
            _loop(0, n_mid, lambda c: med(c).wait())
            _loop(fine, hi, lambda r: one(r).wait())

        _loop(0, N_EXPERTS, pad_group)

        n_chunks = ROW_TILE // ZERO_ROWS
        chunk = lambda c: pltpu.make_async_copy(zbuf, xs_hbm.at[pl.ds(c * ZERO_ROWS, ZERO_ROWS)], pad_sem)
        _loop(nt_ref[0] * n_chunks, (xs_hbm.shape[0] // ROW_TILE) * n_chunks, lambda c: chunk(c).start())
        _loop(nt_ref[0] * n_chunks, (xs_hbm.shape[0] // ROW_TILE) * n_chunks, lambda c: chunk(c).wait())

    def place(src_ref):
        def body(r, k):
            lbuf[slot, lpos_ref[k * TOKEN_TILE + r]] = src_ref[r]
        _for_tile_rows(body)

    @pl.when(i < n_prompt_tiles)
    def _():
        place(xp_ref)

    @pl.when(i >= n_prompt_tiles)
    def _():
        place(xsm_ref)

    def run_copies(tile, s, fn):
        def per_pair(e2):
            for queue in range(2):
                t = tile * N_EXPERTS + e2 * 2 + queue
                src0, dst0 = run_src_ref[t], run_dst_ref[t]

                def per_window(w, src0=src0, dst0=dst0, queue=queue):
                    fn(pltpu.make_async_copy(lbuf.at[s, pl.ds(src0 + w * RUN_WINDOW, RUN_WINDOW)],
                                             xs_hbm.at[pl.ds(dst0 + w * RUN_WINDOW, RUN_WINDOW)], sems.at[s]), queue)

                _loop(0, run_win_ref[t], per_window)

        _loop(0, N_EXPERTS // 2, per_pair)

    @pl.when(i > 0)
    def _():
        run_copies(i - 1, 1 - slot, lambda c, q: c.wait())

    run_copies(i, slot, lambda c, q: c.start(priority=q))

    @pl.when(i == last)
    def _():
        run_copies(i, slot, lambda c, q: c.wait())


def _dispatch(lpos_flat, route, run_dst, run_src, run_win, pad_lo, pad_hi, num_tiles, xn2_p, xn2_s, n_rows):
    row_shape = xn2_p.shape[1:]
    npt = xn2_p.shape[0] // TOKEN_TILE
    nst = xn2_s.shape[0] // TOKEN_TILE
    smem_all = lambda a: pl.BlockSpec(a.shape, lambda i: (0,), memory_space=pltpu.SMEM)
    return pl.pallas_call(
        functools.partial(_dispatch_kernel, npt),
        grid=(npt + nst,),
        in_specs=[pl.BlockSpec((TOKEN_TILE * TOP_K,), lambda i: (i,), memory_space=pltpu.SMEM),
                  smem_all(run_dst), smem_all(run_src), smem_all(run_win),
                  smem_all(pad_lo), smem_all(pad_hi), smem_all(num_tiles),
                  pl.BlockSpec((1, ROUTE_ROWS, TOKEN_TILE), lambda i: (i, 0, 0)),
                  pl.BlockSpec((TOKEN_TILE,) + row_shape, lambda i: (jnp.minimum(i, npt - 1), 0, 0)),
                  pl.BlockSpec((TOKEN_TILE,) + row_shape, lambda i: (jnp.maximum(i - npt, 0), 0, 0))],
        out_specs=(pl.BlockSpec(memory_space=pl.ANY),
                   pl.BlockSpec((1, TOP_K, TOKEN_TILE), lambda i: (i, 0, 0))),
        out_shape=(jax.ShapeDtypeStruct((n_rows,) + row_shape, F32),
                   jax.ShapeDtypeStruct((npt + nst, TOP_K, TOKEN_TILE), I32)),
        scratch_shapes=[pltpu.VMEM((2, LOCAL_ROWS) + row_shape, F32),
                        pltpu.VMEM((ZERO_ROWS,) + row_shape, F32),
                        pltpu.SemaphoreType.DMA((2,))],
        compiler_params=pltpu.CompilerParams(
            dimension_semantics=("arbitrary",), vmem_limit_bytes=VMEM_LIMIT),
        name="dispatch",
    )(lpos_flat, run_dst, run_src, run_win, pad_lo, pad_hi, num_tiles, route, xn2_p, xn2_s)


def _expert_kernel(te_ref, nt_ref, xs_ref, wup_ref, bup_ref, wdn_ref, bdn_ref, ys_ref):
    d_ff = wdn_ref.shape[0]
    used = pl.program_id(0) < nt_ref[0]

    @pl.when(used)
    def _():
        x = xs_ref[...].reshape(ROW_TILE, -1).astype(BF16)
        hid = _dot(x, wup_ref[...].astype(BF16)) + bup_ref[...]
        glu = jnp.minimum(hid[:, :d_ff], SWIGLU_LIMIT)
        lin = jnp.clip(hid[:, d_ff:], -SWIGLU_LIMIT, SWIGLU_LIMIT)
        act = glu * jax.nn.sigmoid(SWIGLU_ALPHA * glu) * (lin + 1.0)
        y = _dot(act.astype(BF16), wdn_ref[...].astype(BF16)) + bdn_ref[...]
        ys_ref[...] = y.reshape(ys_ref.shape)

    @pl.when(jnp.logical_not(used))
    def _():
        ys_ref[...] = jnp.zeros(ys_ref.shape, F32)


def _experts(tile_expert, num_tiles, xs, w_up, b_up, w_down, b_down):
    n_rows = xs.shape[0]
    row_shape = xs.shape[1:]
    n_e, d, d_up = w_up.shape
    d_ff = w_down.shape[1]
    exp_map = lambda i, te, nt: (te[i], 0, 0)
    grid_spec = pltpu.PrefetchScalarGridSpec(
        num_scalar_prefetch=2,
        grid=(n_rows // ROW_TILE,),
        in_specs=[
            pl.BlockSpec((ROW_TILE,) + row_shape, lambda i, te, nt: (jnp.minimum(i, nt[0] - 1), 0, 0)),
            pl.BlockSpec((None, d, d_up), exp_map),
            pl.BlockSpec((None, 1, d_up), exp_map),
            pl.BlockSpec((None, d_ff, d), exp_map),
            pl.BlockSpec((None, 1, d), exp_map),
        ],
        out_specs=pl.BlockSpec((ROW_TILE,) + row_shape, lambda i, te, nt: (i, 0, 0)),
    )
    return pl.pallas_call(
        _expert_kernel,
        grid_spec=grid_spec,
        out_shape=jax.ShapeDtypeStruct((n_rows,) + row_shape, F32),
        compiler_params=pltpu.CompilerParams(
            dimension_semantics=("arbitrary",), vmem_limit_bytes=VMEM_LIMIT),
        name="experts",
    )(tile_expert, num_tiles, xs, w_up, b_up.reshape(n_e, 1, d_up), w_down, b_down.reshape(n_e, 1, d))


def _combine_kernel(n_prompt_tiles, pos_ref, pos_next_ref, ys_hbm, w_ref, h1p_ref, h1s_ref, pp_ref, ps_ref,
                    nple_ref, wg_ref, wp_ref, nfin_ref, yp_ref, ysm_ref, gbuf, sems):
    i = pl.program_id(0)
    is_prompt = i < n_prompt_tiles
    slot = i % 2

    def gather(p_ref, s):
        return lambda r, k: pltpu.make_async_copy(ys_hbm.at[p_ref[k * TOKEN_TILE + r]], gbuf.at[s, k, r],
                                                  sems.at[s])

    @pl.when(i == 0)
    def _():
        copy = gather(pos_ref, 0)
        _for_tile_rows(lambda r, k: copy(r, k).start(priority=k % 2))

    @pl.when(i + 1 < pl.num_programs(0))
    def _():
        copy = gather(pos_next_ref, 1 - slot)
        _for_tile_rows(lambda r, k: copy(r, k).start(priority=k % 2))

    copy = gather(pos_ref, slot)
    _for_tile_rows(lambda r, k: copy(r, k).wait())

    d = h1p_ref.shape[-1]
    wts = jnp.transpose(w_ref[0])
    moe = wts[:, 0:1] * gbuf[slot, 0].reshape(TOKEN_TILE, d)
    for k in range(1, TOP_K):
        moe = moe + wts[:, k:k + 1] * gbuf[slot, k].reshape(TOKEN_TILE, d)
    h2 = jnp.where(is_prompt, h1p_ref[...], h1s_ref[...]) + moe
    p = jnp.where(is_prompt, pp_ref[...], ps_ref[...])
    n3 = _rms(h2, nple_ref[...]).astype(BF16)
    gate = jax.nn.sigmoid(_dot(n3, wg_ref[...]))
    h3 = h2 + gate * _dot(p.astype(BF16), wp_ref[...])
    y = _rms(h3, nfin_ref[...])

    @pl.when(is_prompt)
    def _():
        yp_ref[...] = y

    @pl.when(jnp.logical_not(is_prompt))
    def _():
        ysm_ref[...] = y


def _combine(pos_flat, ys, topw, h1_p, h1_s, p_p, p_s, nple, wg, wp, nfin):
    n_p, d = h1_p.shape
    n_s = h1_s.shape[0]
    ple = p_p.shape[1]
    npt = n_p // TOKEN_TILE
    nst = n_s // TOKEN_TILE
    p_tok = lambda i: (jnp.minimum(i, npt - 1), 0)
    s_tok = lambda i: (jnp.maximum(i - npt, 0), 0)
    const = lambda i: (0, 0)
    last = npt + nst - 1
    this_tile = pl.BlockSpec((TOKEN_TILE * TOP_K,), lambda i: (i,), memory_space=pltpu.SMEM)
    next_tile = pl.BlockSpec((TOKEN_TILE * TOP_K,), lambda i: (jnp.minimum(i + 1, last),), memory_space=pltpu.SMEM)
    return pl.pallas_call(
        functools.partial(_combine_kernel, npt),
        grid=(npt + nst,),
        in_specs=[
            this_tile, next_tile,
            pl.BlockSpec(memory_space=pl.ANY),
            pl.BlockSpec((1, WEIGHT_ROWS, TOKEN_TILE), lambda i: (i, 0, 0)),
            pl.BlockSpec((TOKEN_TILE, d), p_tok),
            pl.BlockSpec((TOKEN_TILE, d), s_tok),
            pl.BlockSpec((TOKEN_TILE, ple), p_tok),
            pl.BlockSpec((TOKEN_TILE, ple), s_tok),
            pl.BlockSpec((1, d), const),
            pl.BlockSpec((d, d), const),
            pl.BlockSpec((ple, d), const),
            pl.BlockSpec((1, d), const),
        ],
        out_specs=(pl.BlockSpec((TOKEN_TILE, d), p_tok), pl.BlockSpec((TOKEN_TILE, d), s_tok)),
        out_shape=(jax.ShapeDtypeStruct((n_p, d), F32), jax.ShapeDtypeStruct((n_s, d), F32)),
        scratch_shapes=[pltpu.VMEM((2, TOP_K, TOKEN_TILE) + ys.shape[1:], F32), pltpu.SemaphoreType.DMA((2,))],
        compiler_params=pltpu.CompilerParams(
            dimension_semantics=("arbitrary",), vmem_limit_bytes=VMEM_LIMIT),
        name="combine",
    )(pos_flat, pos_flat, ys, topw, h1_p, h1_s, p_p, p_s, nple, wg, wp, nfin)


def kernel(x_prompt, x_sample, cache_pool, p_prompt, p_sample, norm_mix, w_in, ln_v_g, ln_v_b, w_spatial, b_spatial, w_pool_group, pool_scale, w_branch_a, w_branch_b, w_out, norm_moe, w_router, b_router, w_up, b_up, w_down, b_down, norm_ple, w_ple_gate, w_ple_proj, norm_final):
    assert norm_mix.shape[0] == 1, "single-layer stack"
    nb, seq, d = x_prompt.shape
    sb, sseq, _ = x_sample.shape
    n_prompt = nb * seq
    n_sample = sb * sseq
    n_tok = n_prompt + n_sample
    past_len = 4096
    row = lambda v: v.reshape(1, -1)

    wr = jnp.transpose(w_router[0]).astype(BF16)
    br = b_router[0].reshape(N_EXPERTS, 1)
    weights_a = (row(norm_mix[0]), w_in[0].astype(BF16), row(ln_v_g[0]), row(ln_v_b[0]), w_spatial[0],
                 jnp.transpose(b_spatial[0]), w_pool_group[0].astype(BF16), row(pool_scale[0]),
                 w_branch_a[0].astype(BF16), w_branch_b[0].astype(BF16), w_out[0].astype(BF16),
                 row(norm_moe[0]), wr, br)
    h1_p, xn2_p, route_p, rw_p, cnt_p, pool_p = _stage_a_prompt(x_prompt, weights_a)
    h1_s, xn2_s, route_s, rw_s, cnt_s, pool_s, vn_s = _stage_a_sample(x_sample, cache_pool[0], past_len, weights_a)

    n_tab = jnp.concatenate([cnt_p[:, :, 0], cnt_s[:, :, 0]], axis=0).astype(I32)
    ahead = jnp.cumsum(n_tab, axis=0) - n_tab
    total = jnp.sum(n_tab, axis=0)
    tiles_e = (total + (RUN_WINDOW - 1) + (ROW_TILE - 1)) // ROW_TILE
    ends = jnp.cumsum(tiles_e)
    first = (ends - tiles_e) * ROW_TILE
    num_tiles = ends[-1:]
    max_tiles = (n_tok * TOP_K + N_EXPERTS * (RUN_WINDOW - 1)) // ROW_TILE + N_EXPERTS
    tile_ids = jnp.minimum(jnp.arange(max_tiles, dtype=I32), num_tiles[0] - 1)
    tile_expert = jnp.sum((tile_ids[:, None] >= ends[None, :]).astype(I32), axis=1)
    run_dst = (first[None, :] + ahead).reshape(-1)
    run_win = (n_tab + (RUN_WINDOW - 1)) // RUN_WINDOW
    run_src = ((jnp.cumsum(run_win, axis=1) - run_win) * RUN_WINDOW).reshape(-1)
    pad_lo = first + total
    pad_hi = ends * ROW_TILE

    route = jnp.concatenate([route_p, route_s], axis=0)
    topw = jnp.concatenate([rw_p, rw_s], axis=0)
    lpos_flat = route[:, :TOP_K].reshape(-1)

    xs, pos = _dispatch(lpos_flat, route, run_dst, run_src, run_win.reshape(-1), pad_lo, pad_hi, num_tiles,
                        xn2_p, xn2_s, max_tiles * ROW_TILE)
    ys = _experts(tile_expert, num_tiles, xs, w_up[0], b_up[0], w_down[0], b_down[0])

    y_p, y_s = _combine(pos.reshape(-1), ys, topw, h1_p, h1_s, p_prompt[0].reshape(n_prompt, -1),
                        p_sample[0].reshape(n_sample, -1), row(norm_ple[0]), w_ple_gate[0].astype(BF16),
                        w_ple_proj[0].astype(BF16), row(norm_final))

    return (y_p.reshape(nb, seq, d), y_s.reshape(sb, sseq, d), pool_p[None], pool_s[None],
            vn_s.reshape(1, sb, sseq, d))
```

```python
import functools

import jax
import jax.numpy as jnp
from jax import lax
from jax.experimental import pallas as pl
from jax.experimental.pallas import tpu as pltpu

F32 = jnp.float32
BF16 = jnp.bfloat16
I32 = jnp.int32

EPS = 1e-6
CHUNK = 64
GMLP_BLOCK = 128
GMLP_HEADS = 4
POOL_WINDOWS = (2, 4, 8, 16)
POOL_STATE = 15
SUBLANES = 8
HALO = SUBLANES * len(POOL_WINDOWS)
N_EXPERTS = 32
TOP_K = 4
SWIGLU_LIMIT = 7.0
SWIGLU_ALPHA = 1.702
LANES = 128

STAGE_TILE = 512
TOKEN_TILE = 256
ROW_TILE = 384
RUN_WINDOW = 16
LOCAL_ROWS = N_EXPERTS * RUN_WINDOW + TOKEN_TILE * TOP_K
ZERO_ROWS = 64
VMEM_LIMIT = 56 * 1024 * 1024


def _rms(xf, g):
    return xf * lax.rsqrt(jnp.mean(xf * xf, axis=-1, keepdims=True) + EPS) * g


def _dot(a, b):
    return jnp.dot(a, b, preferred_element_type=F32)


ROUTE_ROWS = 2 * TOP_K
WEIGHT_ROWS = 2 * TOP_K


def _route_tile(logits):
    ne, nt = logits.shape
    eid = lax.broadcasted_iota(I32, (ne, nt), 0).astype(F32)
    neg = jnp.float32(-jnp.inf)
    lg = logits
    picked = jnp.zeros((ne, nt), F32)
    vals, sels = [], []
    for _ in range(TOP_K):
        m = jnp.max(lg, axis=0, keepdims=True)
        idx = jnp.min(jnp.where(lg == m, eid, float(ne)), axis=0, keepdims=True)
        sel = eid == idx
        lg = jnp.where(sel, neg, lg)
        picked = picked + sel.astype(F32)
        vals.append(m)
        sels.append(sel)
    ex = [jnp.exp(v - vals[0]) for v in vals]
    den = ex[0] + ex[1] + ex[2] + ex[3]

    earlier = (lax.broadcasted_iota(I32, (nt, nt), 0) < lax.broadcasted_iota(I32, (nt, nt), 1)).astype(BF16)
    before = _dot(picked.astype(BF16), earlier)
    counts = jnp.sum(picked, axis=1, keepdims=True)
    windows = jnp.floor((counts + (RUN_WINDOW - 1)) * (1.0 / RUN_WINDOW))
    lower = (lax.broadcasted_iota(I32, (ne, ne), 0) > lax.broadcasted_iota(I32, (ne, ne), 1)).astype(BF16)
    run_start = _dot(lower, jnp.broadcast_to(windows, (ne, LANES)).astype(BF16))[:, 0:1] * float(RUN_WINDOW)

    zero_row = jnp.zeros((1, nt), F32)
    places = [jnp.sum(jnp.where(s, run_start + before, 0.0), axis=0, keepdims=True) for s in sels]
    packed = jnp.concatenate(places + [zero_row] * (ROUTE_ROWS - TOP_K), axis=0)
    wts = jnp.concatenate([e / den for e in ex] + [zero_row] * (WEIGHT_ROWS - TOP_K), axis=0)
    return packed.astype(I32), wts, counts


def _stage_a_kernel(n_seq, seq_rows, carry_halo, pos0, *refs):
    if carry_halo:
        (x_ref, nm_ref, win_ref, lng_ref, lnb_ref, wsp_ref, bsp_ref, wpool_ref, pscale_ref,
         wa_ref, wb_ref, wout_ref, nmoe_ref, wr_ref, br_ref,
         h1_ref, xn2_ref, route_ref, rw_ref, cnt_ref, pool_ref, vn_ref, xc_ref, lv_ref, a_ref) = refs
        cache_ref = None
    else:
        (x_ref, cache_ref, nm_ref, win_ref, lng_ref, lnb_ref, wsp_ref, bsp_ref, wpool_ref,
         pscale_ref, wa_ref, wb_ref, wout_ref, nmoe_ref, wr_ref, br_ref,
         h1_ref, xn2_ref, route_ref, rw_ref, cnt_ref, pool_ref, vn_ref, xc_ref, lv_ref, a_ref) = refs
    d = x_ref.shape[-1]
    head_dim = d // GMLP_HEADS
    group_dim = d // len(POOL_WINDOWS)
    blk = min(GMLP_BLOCK, seq_rows)
    j = pl.program_id(1) if carry_halo else 0

    x = x_ref[...]
    n = _rms(x, nm_ref[...]).astype(BF16)

    def proj(seg):
        return _dot(n, win_ref[:, seg * d:(seg + 1) * d])

    v = jax.nn.gelu(proj(1))
    mu = jnp.mean(v, axis=-1, keepdims=True)
    vc = v - mu
    vn_ref[...] = vc * lax.rsqrt(jnp.mean(vc * vc, axis=-1, keepdims=True) + EPS) * lng_ref[...] + lnb_ref[...]
    u = jax.nn.gelu(proj(0))

    ri = lax.broadcasted_iota(I32, (GMLP_BLOCK, GMLP_BLOCK), 0) // CHUNK
    ci = lax.broadcasted_iota(I32, (GMLP_BLOCK, GMLP_BLOCK), 1) // CHUNK
    causal = (ri >= ci).astype(F32)
    for h in range(GMLP_HEADS):
        wm = (wsp_ref[h] * causal)[:blk, :blk].astype(BF16)
        bias = bsp_ref[:blk, h:h + 1]
        cols = slice(h * head_dim, (h + 1) * head_dim)
        for r0 in range(0, n_seq * seq_rows, blk):
            rows = slice(r0, r0 + blk)
            sg = _dot(wm, vn_ref[rows, cols].astype(BF16)) + bias
            a_ref[rows, cols] = (u[rows, cols] * sg).astype(BF16)
    y_a = _dot(a_ref[...], wa_ref[...])

    pb = proj(2)
    if carry_halo:
        @pl.when(j == 0)
        def _():
            xc_ref[0, 0:HALO, :] = jnp.zeros((HALO, d), F32)
    else:
        for s in range(n_seq):
            xc_ref[s, 0:HALO, :] = jnp.zeros((HALO, d), F32)
            xc_ref[s, HALO - POOL_STATE:HALO, :] = cache_ref[s]
    for s in range(n_seq):
        xc_ref[s, HALO:HALO + seq_rows, :] = pb[s * seq_rows:(s + 1) * seq_rows]

    end = HALO + seq_rows
    window_sums = [[] for _ in POOL_WINDOWS]
    for s in range(n_seq):
        src, shift = xc_ref, 1
        for lvl in range(len(POOL_WINDOWS)):
            lo = SUBLANES * (lvl + 1)
            cols = slice(lvl * group_dim, d)
            summed = src[s, lo:end, cols] + src[s, lo - shift:end - shift, cols]
            window_sums[lvl].append(summed[HALO - lo:, :group_dim])
            if lvl + 1 < len(POOL_WINDOWS):
                dst = lv_ref.at[lvl % 2]
                dst[s, lo:end, (lvl + 1) * group_dim:d] = summed[:, group_dim:]
                src, shift = dst, 2 * shift

    if carry_halo:
        pos = pos0 + j * seq_rows + lax.broadcasted_iota(I32, (seq_rows, 1), 0)
    z_groups = []
    for g, w in enumerate(POOL_WINDOWS):
        cols = slice(g * group_dim, (g + 1) * group_dim)
        if carry_halo:
            cnt = jnp.minimum(pos + 1, w).astype(F32)
        else:
            cnt = float(min(pos0 + 1, w))
        parts = [window_sums[g][s] / cnt - xc_ref[s, HALO:end, cols] for s in range(n_seq)]
        zg = parts[0] if n_seq == 1 else jnp.concatenate(parts, axis=0)
        z_groups.append(_dot(zg.astype(BF16), wpool_ref[g]) * pscale_ref[:, cols])
    z = jnp.concatenate(z_groups, axis=1)
    y_b = _dot(z.astype(BF16), wb_ref[...])

    for s in range(n_seq):
        tail = xc_ref[s, HALO + seq_rows - POOL_STATE:HALO + seq_rows, :]
        if carry_halo:
            @pl.when(j == pl.num_programs(1) - 1)
            def _():
                pool_ref[...] = tail
        else:
            pool_ref[s] = tail
    if carry_halo:
        xc_ref[0, 0:HALO, :] = xc_ref[0, seq_rows:seq_rows + HALO, :]

    g_a = jax.nn.sigmoid(proj(3))
    g_b = jax.nn.sigmoid(proj(4))
    merged = (g_a * y_a + g_b * y_b).astype(BF16)
    h1 = x + _dot(merged, wout_ref[...])
    h1_ref[...] = h1

    xn2 = _rms(h1, nmoe_ref[...]).astype(BF16)
    xn2_ref[...] = xn2.astype(F32).reshape(xn2_ref.shape)
    for t in range(n_seq * seq_rows // TOKEN_TILE):
        rows = slice(t * TOKEN_TILE, (t + 1) * TOKEN_TILE)
        logits = lax.dot_general(wr_ref[...], xn2[rows], (((1,), (1,)), ((), ())),
                                 preferred_element_type=F32) + br_ref[...]
        packed, wts, counts = _route_tile(logits)
        route_ref[t] = packed
        rw_ref[t] = wts
        cnt_ref[t] = jnp.broadcast_to(counts, (N_EXPERTS, LANES))


def _const_spec(shape, grid_rank):
    zeros = (0,) * len(shape)
    if grid_rank == 2:
        return pl.BlockSpec(shape, lambda b, j: zeros, pipeline_mode=pl.Buffered(1))
    return pl.BlockSpec(shape, lambda i: zeros, pipeline_mode=pl.Buffered(1))


def _stage_a_outputs(n_rows, d):
    return (jax.ShapeDtypeStruct((n_rows, d), F32),
            jax.ShapeDtypeStruct((n_rows, SUBLANES, d // SUBLANES), F32),
            jax.ShapeDtypeStruct((n_rows // TOKEN_TILE, ROUTE_ROWS, TOKEN_TILE), I32),
            jax.ShapeDtypeStruct((n_rows // TOKEN_TILE, WEIGHT_ROWS, TOKEN_TILE), F32),
            jax.ShapeDtypeStruct((n_rows // TOKEN_TILE, N_EXPERTS, LANES), F32))


def _stage_a_prompt(x_prompt, weights):
    nb, seq, d = x_prompt.shape
    tiles_per_seq = seq // STAGE_TILE
    sub = STAGE_TILE // TOKEN_TILE
    tok_map = lambda b, j: (b * tiles_per_seq + j, 0)
    tok_map3 = lambda b, j: (b * tiles_per_seq + j, 0, 0)
    return pl.pallas_call(
        functools.partial(_stage_a_kernel, 1, STAGE_TILE, True, 0),
        grid=(nb, tiles_per_seq),
        in_specs=[pl.BlockSpec((None, STAGE_TILE, d), lambda b, j: (b, j, 0))]
        + [_const_spec(w.shape, 2) for w in weights],
        out_specs=(
            pl.BlockSpec((STAGE_TILE, d), tok_map),
            pl.BlockSpec((STAGE_TILE, SUBLANES, d // SUBLANES), tok_map3),
            pl.BlockSpec((sub, ROUTE_ROWS, TOKEN_TILE), tok_map3),
            pl.BlockSpec((sub, WEIGHT_ROWS, TOKEN_TILE), tok_map3),
            pl.BlockSpec((sub, N_EXPERTS, LANES), tok_map3),
            pl.BlockSpec((None, POOL_STATE, d), lambda b, j: (b, 0, 0)),
        ),
        out_shape=_stage_a_outputs(nb * seq, d) + (jax.ShapeDtypeStruct((nb, POOL_STATE, d), F32),),
        scratch_shapes=[
            pltpu.VMEM((STAGE_TILE, d), F32),
            pltpu.VMEM((1, HALO + STAGE_TILE, d), F32),
            pltpu.VMEM((2, 1, HALO + STAGE_TILE, d), F32),
            pltpu.VMEM((STAGE_TILE, d), BF16),
        ],
        compiler_params=pltpu.CompilerParams(
            dimension_semantics=("arbitrary", "arbitrary"), vmem_limit_bytes=VMEM_LIMIT),
        name="stage_a_prompt",
    )(x_prompt, *weights)


def _stage_a_sample(x_sample, cache, past_len, weights):
    sb, sseq, d = x_sample.shape
    n_s = sb * sseq
    whole2 = lambda i: (0, 0)
    whole3 = lambda i: (0, 0, 0)
    return pl.pallas_call(
        functools.partial(_stage_a_kernel, sb, sseq, False, past_len),
        grid=(1,),
        in_specs=[pl.BlockSpec((n_s, d), whole2), pl.BlockSpec((sb, POOL_STATE, d), whole3)]
        + [_const_spec(w.shape, 1) for w in weights],
        out_specs=(
            pl.BlockSpec((n_s, d), whole2),
            pl.BlockSpec((n_s, SUBLANES, d // SUBLANES), whole3),
            pl.BlockSpec((n_s // TOKEN_TILE, ROUTE_ROWS, TOKEN_TILE), whole3),
            pl.BlockSpec((n_s // TOKEN_TILE, WEIGHT_ROWS, TOKEN_TILE), whole3),
            pl.BlockSpec((n_s // TOKEN_TILE, N_EXPERTS, LANES), whole3),
            pl.BlockSpec((sb, POOL_STATE, d), whole3),
            pl.BlockSpec((n_s, d), whole2),
        ),
        out_shape=_stage_a_outputs(n_s, d) + (jax.ShapeDtypeStruct((sb, POOL_STATE, d), F32),
                                              jax.ShapeDtypeStruct((n_s, d), F32)),
        scratch_shapes=[
            pltpu.VMEM((sb, HALO + sseq, d), F32),
            pltpu.VMEM((2, sb, HALO + sseq, d), F32),
            pltpu.VMEM((n_s, d), BF16),
        ],
        compiler_params=pltpu.CompilerParams(
            dimension_semantics=("arbitrary",), vmem_limit_bytes=VMEM_LIMIT),
        name="stage_a_sample",
    )(x_sample.reshape(n_s, d), cache, *weights)


def _for_tile_rows(fn):
    def body(g, c):
        for rr in range(SUBLANES):
            for k in range(TOP_K):
                fn(g * SUBLANES + rr, k)
        return c

    lax.fori_loop(0, TOKEN_TILE // SUBLANES, body, 0)


def _loop(lo, hi, fn):
    def body(i, c):
        fn(i)
        return c

    lax.fori_loop(lo, hi, body, 0)


def _for_run_windows(run_src_ref, run_dst_ref, run_win_ref, tile, fn):
    def per_pair(e2):
        for queue in range(2):
            t = tile * N_EXPERTS + e2 * 2 + queue
            src0, dst0 = run_src_ref[t], run_dst_ref[t]
            _loop(0, run_win_ref[t],
                  lambda w, src0=src0, dst0=dst0, queue=queue: fn(src0 + w * RUN_WINDOW, dst0 + w * RUN_WINDOW, queue))

    _loop(0, N_EXPERTS // 2, per_pair)


def _dispatch_kernel(n_prompt_tiles, lpos_ref, run_dst_ref, run_src_ref, run_win_ref, pad_lo_ref, pad_hi_ref,
                     nt_ref, xp_ref, xsm_ref, xs_hbm, lbuf, zbuf, sems):
    i = pl.program_id(0)
    last = pl.num_programs(0) - 1
    slot = i % 2

    @pl.when(i == 0)
    def _():
        lbuf[...] = jnp.zeros(lbuf.shape, F32)
        zbuf[...] = jnp.zeros(zbuf.shape, F32)
        pad_sem = sems.at[0]

        def pad_group(e):
            lo, hi = pad_lo_ref[e], pad_hi_ref[e]
            n_big = (hi - lo) // ZERO_ROWS
            mid = lo + n_big * ZERO_ROWS
            n_mid = (hi - mid) // SUBLANES
            fine = mid + n_mid * SUBLANES
            big = lambda c: pltpu.make_async_copy(zbuf, xs_hbm.at[pl.ds(lo + c * ZERO_ROWS, ZERO_ROWS)], pad_sem)
            med = lambda c: pltpu.make_async_copy(zbuf.at[pl.ds(0, SUBLANES)],
                                                  xs_hbm.at[pl.ds(mid + c * SUBLANES, SUBLANES)], pad_sem)
            one = lambda r: pltpu.make_async_copy(zbuf.at[0], xs_hbm.at[r], pad_sem)
            _loop(0, n_big, lambda c: big(c).start())
            _loop(0, n_mid, lambda c: med(c).start())
            _loop(fine, hi, lambda r: one(r).start())
            _loop(0, n_big, lambda c: big(c).wait())
            _loop(0, n_mid, lambda c: med(c).wait())
            _loop(fine, hi, lambda r: one(r).wait())

        _loop(0, N_EXPERTS, pad_group)

        n_chunks = ROW_TILE // ZERO_ROWS
        chunk = lambda c: pltpu.make_async_copy(zbuf, xs_hbm.at[pl.ds(c * ZERO_ROWS, ZERO_ROWS)], pad_sem)
        _loop(nt_ref[0] * n_chunks, (xs_hbm.shape[0] // ROW_TILE) * n_chunks, lambda c: chunk(c).start())
        _loop(nt_ref[0] * n_chunks, (xs_hbm.shape[0] // ROW_TILE) * n_chunks, lambda c: chunk(c).wait())

    def place(src_ref):
        def body(r, k):
            lbuf[slot, lpos_ref[k * TOKEN_TILE + r]] = src_ref[r]
        _for_tile_rows(body)

    @pl.when(i < n_prompt_tiles)
    def _():
        place(xp_ref)

    @pl.when(i >= n_prompt_tiles)
    def _():
        place(xsm_ref)

    def run_copies(tile, s, fn):
        def window(buf_row, sorted_row, queue):
            fn(pltpu.make_async_copy(lbuf.at[s, pl.ds(buf_row, RUN_WINDOW)],
                                     xs_hbm.at[pl.ds(sorted_row, RUN_WINDOW)], sems.at[s]), queue)

        _for_run_windows(run_src_ref, run_dst_ref, run_win_ref, tile, window)

    @pl.when(i > 0)
    def _():
        run_copies(i - 1, 1 - slot, lambda c, q: c.wait())

    run_copies(i, slot, lambda c, q: c.start(priority=q))

    @pl.when(i == last)
    def _():
        run_copies(i, slot, lambda c, q: c.wait())


def _dispatch(lpos_flat, run_dst, run_src, run_win, pad_lo, pad_hi, num_tiles, xn2_p, xn2_s, n_rows):
    row_shape = xn2_p.shape[1:]
    npt = xn2_p.shape[0] // TOKEN_TILE
    nst = xn2_s.shape[0] // TOKEN_TILE
    smem_all = lambda a: pl.BlockSpec(a.shape, lambda i: (0,), memory_space=pltpu.SMEM)
    return pl.pallas_call(
        functools.partial(_dispatch_kernel, npt),
        grid=(npt + nst,),
        in_specs=[pl.BlockSpec((TOKEN_TILE * TOP_K,), lambda i: (i,), memory_space=pltpu.SMEM),
                  smem_all(run_dst), smem_all(run_src), smem_all(run_win),
                  smem_all(pad_lo), smem_all(pad_hi), smem_all(num_tiles),
                  pl.BlockSpec((TOKEN_TILE,) + row_shape, lambda i: (jnp.minimum(i, npt - 1), 0, 0)),
                  pl.BlockSpec((TOKEN_TILE,) + row_shape, lambda i: (jnp.maximum(i - npt, 0), 0, 0))],
        out_specs=pl.BlockSpec(memory_space=pl.ANY),
        out_shape=jax.ShapeDtypeStruct((n_rows,) + row_shape, F32),
        scratch_shapes=[pltpu.VMEM((2, LOCAL_ROWS) + row_shape, F32),
                        pltpu.VMEM((ZERO_ROWS,) + row_shape, F32),
                        pltpu.SemaphoreType.DMA((2,))],
        compiler_params=pltpu.CompilerParams(
            dimension_semantics=("arbitrary",), vmem_limit_bytes=VMEM_LIMIT),
        name="dispatch",
    )(lpos_flat, run_dst, run_src, run_win, pad_lo, pad_hi, num_tiles, xn2_p, xn2_s)


def _expert_kernel(te_ref, nt_ref, xs_ref, wup_ref, bup_ref, wdn_ref, bdn_ref, ys_ref):
    d_ff = wdn_ref.shape[0]
    used = pl.program_id(0) < nt_ref[0]

    @pl.when(used)
    def _():
        x = xs_ref[...].reshape(ROW_TILE, -1).astype(BF16)
        hid = _dot(x, wup_ref[...].astype(BF16)) + bup_ref[...]
        glu = jnp.minimum(hid[:, :d_ff], SWIGLU_LIMIT)
        lin = jnp.clip(hid[:, d_ff:], -SWIGLU_LIMIT, SWIGLU_LIMIT)
        act = glu * jax.nn.sigmoid(SWIGLU_ALPHA * glu) * (lin + 1.0)
        y = _dot(act.astype(BF16), wdn_ref[...].astype(BF16)) + bdn_ref[...]
        ys_ref[...] = y.reshape(ys_ref.shape)

    @pl.when(jnp.logical_not(used))
    def _():
        ys_ref[...] = jnp.zeros(ys_ref.shape, F32)


def _experts(tile_expert, num_tiles, xs, w_up, b_up, w_down, b_down):
    n_rows = xs.shape[0]
    row_shape = xs.shape[1:]
    n_e, d, d_up = w_up.shape
    d_ff = w_down.shape[1]
    exp_map = lambda i, te, nt: (te[i], 0, 0)
    grid_spec = pltpu.PrefetchScalarGridSpec(
        num_scalar_prefetch=2,
        grid=(n_rows // ROW_TILE,),
        in_specs=[
            pl.BlockSpec((ROW_TILE,) + row_shape, lambda i, te, nt: (jnp.minimum(i, nt[0] - 1), 0, 0)),
            pl.BlockSpec((None, d, d_up), exp_map),
            pl.BlockSpec((None, 1, d_up), exp_map),
            pl.BlockSpec((None, d_ff, d), exp_map),
            pl.BlockSpec((None, 1, d), exp_map),
        ],
        out_specs=pl.BlockSpec((ROW_TILE,) + row_shape, lambda i, te, nt: (i, 0, 0)),
    )
    return pl.pallas_call(
        _expert_kernel,
        grid_spec=grid_spec,
        out_shape=jax.ShapeDtypeStruct((n_rows,) + row_shape, F32),
        compiler_params=pltpu.CompilerParams(
            dimension_semantics=("arbitrary",), vmem_limit_bytes=VMEM_LIMIT),
        name="experts",
    )(tile_expert, num_tiles, xs, w_up, b_up.reshape(n_e, 1, d_up), w_down, b_down.reshape(n_e, 1, d))


def _combine_kernel(n_prompt_tiles, lpos_ref, run_dst_ref, run_src_ref, run_win_ref, ys_hbm, w_ref,
                    h1p_ref, h1s_ref, pp_ref, ps_ref, nple_ref, wg_ref, wp_ref, nfin_ref,
                    yp_ref, ysm_ref, lbuf, gbuf, sems):
    i = pl.program_id(0)
    is_prompt = i < n_prompt_tiles
    slot = i % 2

    def run_copies(tile, s, fn):
        def window(buf_row, sorted_row, queue):
            fn(pltpu.make_async_copy(ys_hbm.at[pl.ds(sorted_row, RUN_WINDOW)],
                                     lbuf.at[s, pl.ds(buf_row, RUN_WINDOW)], sems.at[s]), queue)

        _for_run_windows(run_src_ref, run_dst_ref, run_win_ref, tile, window)

    @pl.when(i == 0)
    def _():
        run_copies(i, 0, lambda c, q: c.start(priority=q))

    @pl.when(i + 1 < pl.num_programs(0))
    def _():
        run_copies(i + 1, 1 - slot, lambda c, q: c.start(priority=q))

    run_copies(i, slot, lambda c, q: c.wait())

    def pick(r, k):
        gbuf[k, r] = lbuf[slot, lpos_ref[k * TOKEN_TILE + r]]
    _for_tile_rows(pick)

    d = h1p_ref.shape[-1]
    wts = jnp.transpose(w_ref[0])
    moe = wts[:, 0:1] * gbuf[0].reshape(TOKEN_TILE, d)
    for k in range(1, TOP_K):
        moe = moe + wts[:, k:k + 1] * gbuf[k].reshape(TOKEN_TILE, d)
    h2 = jnp.where(is_prompt, h1p_ref[...], h1s_ref[...]) + moe
    p = jnp.where(is_prompt, pp_ref[...], ps_ref[...])
    n3 = _rms(h2, nple_ref[...]).astype(BF16)
    gate = jax.nn.sigmoid(_dot(n3, wg_ref[...]))
    h3 = h2 + gate * _dot(p.astype(BF16), wp_ref[...])
    y = _rms(h3, nfin_ref[...])

    @pl.when(is_prompt)
    def _():
        yp_ref[...] = y

    @pl.when(jnp.logical_not(is_prompt))
    def _():
        ysm_ref[...] = y


def _combine(lpos_flat, run_dst, run_src, run_win, ys, topw, h1_p, h1_s, p_p, p_s, nple, wg, wp, nfin):
    n_p, d = h1_p.shape
    n_s = h1_s.shape[0]
    ple = p_p.shape[1]
    npt = n_p // TOKEN_TILE
    nst = n_s // TOKEN_TILE
    row_shape = ys.shape[1:]
    p_tok = lambda i: (jnp.minimum(i, npt - 1), 0)
    s_tok = lambda i: (jnp.maximum(i - npt, 0), 0)
    const = lambda i: (0, 0)
    smem_all = lambda a: pl.BlockSpec(a.shape, lambda i: (0,), memory_space=pltpu.SMEM)
    return pl.pallas_call(
        functools.partial(_combine_kernel, npt),
        grid=(npt + nst,),
        in_specs=[
            pl.BlockSpec((TOKEN_TILE * TOP_K,), lambda i: (i,), memory_space=pltpu.SMEM),
            smem_all(run_dst), smem_all(run_src), smem_all(run_win),
            pl.BlockSpec(memory_space=pl.ANY),
            pl.BlockSpec((1, WEIGHT_ROWS, TOKEN_TILE), lambda i: (i, 0, 0)),
            pl.BlockSpec((TOKEN_TILE, d), p_tok),
            pl.BlockSpec((TOKEN_TILE, d), s_tok),
            pl.BlockSpec((TOKEN_TILE, ple), p_tok),
            pl.BlockSpec((TOKEN_TILE, ple), s_tok),
            pl.BlockSpec((1, d), const),
            pl.BlockSpec((d, d), const),
            pl.BlockSpec((ple, d), const),
            pl.BlockSpec((1, d), const),
        ],
        out_specs=(pl.BlockSpec((TOKEN_TILE, d), p_tok), pl.BlockSpec((TOKEN_TILE, d), s_tok)),
        out_shape=(jax.ShapeDtypeStruct((n_p, d), F32), jax.ShapeDtypeStruct((n_s, d), F32)),
        scratch_shapes=[pltpu.VMEM((2, LOCAL_ROWS) + row_shape, F32),
                        pltpu.VMEM((TOP_K, TOKEN_TILE) + row_shape, F32),
                        pltpu.SemaphoreType.DMA((2,))],
        compiler_params=pltpu.CompilerParams(
            dimension_semantics=("arbitrary",), vmem_limit_bytes=VMEM_LIMIT),
        name="combine",
    )(lpos_flat, run_dst, run_src, run_win, ys, topw, h1_p, h1_s, p_p, p_s, nple, wg, wp, nfin)


def kernel(x_prompt, x_sample, cache_pool, p_prompt, p_sample, norm_mix, w_in, ln_v_g, ln_v_b, w_spatial, b_spatial, w_pool_group, pool_scale, w_branch_a, w_branch_b, w_out, norm_moe, w_router, b_router, w_up, b_up, w_down, b_down, norm_ple, w_ple_gate, w_ple_proj, norm_final):
    assert norm_mix.shape[0] == 1, "single-layer stack"
    nb, seq, d = x_prompt.shape
    sb, sseq, _ = x_sample.shape
    n_prompt = nb * seq
    n_sample = sb * sseq
    n_tok = n_prompt + n_sample
    past_len = 4096
    row = lambda v: v.reshape(1, -1)

    wr = jnp.transpose(w_router[0]).astype(BF16)
    br = b_router[0].reshape(N_EXPERTS, 1)
    weights_a = (row(norm_mix[0]), w_in[0].astype(BF16), row(ln_v_g[0]), row(ln_v_b[0]), w_spatial[0],
                 jnp.transpose(b_spatial[0]), w_pool_group[0].astype(BF16), row(pool_scale[0]),
                 w_branch_a[0].astype(BF16), w_branch_b[0].astype(BF16), w_out[0].astype(BF16),
                 row(norm_moe[0]), wr, br)
    h1_p, xn2_p, route_p, rw_p, cnt_p, pool_p = _stage_a_prompt(x_prompt, weights_a)
    h1_s, xn2_s, route_s, rw_s, cnt_s, pool_s, vn_s = _stage_a_sample(x_sample, cache_pool[0], past_len, weights_a)

    n_tab = jnp.concatenate([cnt_p[:, :, 0], cnt_s[:, :, 0]], axis=0).astype(I32)
    ahead = jnp.cumsum(n_tab, axis=0) - n_tab
    total = jnp.sum(n_tab, axis=0)
    tiles_e = (total + (RUN_WINDOW - 1) + (ROW_TILE - 1)) // ROW_TILE
    ends = jnp.cumsum(tiles_e)
    first = (ends - tiles_e) * ROW_TILE
    num_tiles = ends[-1:]
    max_tiles = (n_tok * TOP_K + N_EXPERTS * (RUN_WINDOW - 1)) // ROW_TILE + N_EXPERTS
    tile_ids = jnp.minimum(jnp.arange(max_tiles, dtype=I32), num_tiles[0] - 1)
    tile_expert = jnp.sum((tile_ids[:, None] >= ends[None, :]).astype(I32), axis=1)
    run_dst = (first[None, :] + ahead).reshape(-1)
    run_win2 = (n_tab + (RUN_WINDOW - 1)) // RUN_WINDOW
    run_src = ((jnp.cumsum(run_win2, axis=1) - run_win2) * RUN_WINDOW).reshape(-1)
    run_win = run_win2.reshape(-1)
    pad_lo = first + total
    pad_hi = ends * ROW_TILE

    lpos_flat = jnp.concatenate([route_p[:, :TOP_K], route_s[:, :TOP_K]], axis=0).reshape(-1)
    topw = jnp.concatenate([rw_p, rw_s], axis=0)

    xs = _dispatch(lpos_flat, run_dst, run_src, run_win, pad_lo, pad_hi, num_tiles, xn2_p, xn2_s,
                   max_tiles * ROW_TILE)
    ys = _experts(tile_expert, num_tiles, xs, w_up[0], b_up[0], w_down[0], b_down[0])

    y_p, y_s = _combine(lpos_flat, run_dst, run_src, run_win, ys, topw, h1_p, h1_s,
                        p_prompt[0].reshape(n_prompt, -1),
                        p_sample[0].reshape(n_sample, -1), row(norm_ple[0]), w_ple_gate[0].astype(BF16),
                        w_ple_proj[0].astype(BF16), row(norm_final))

    return (y_p.reshape(nb, seq, d), y_s.reshape(sb, sseq, d), pool_p[None], pool_s[None],
            vn_s.reshape(1, sb, sseq, d))
```

```python
import functools

import jax
import jax.numpy as jnp
from jax import lax
from jax.experimental import pallas as pl
from jax.experimental.pallas import tpu as pltpu

F32 = jnp.float32
BF16 = jnp.bfloat16
I32 = jnp.int32

EPS = 1e-6
CHUNK = 64
GMLP_BLOCK = 128
GMLP_HEADS = 4
POOL_WINDOWS = (2, 4, 8, 16)
POOL_STATE = 15
SUBLANES = 8
HALO = SUBLANES * len(POOL_WINDOWS)
N_EXPERTS = 32
TOP_K = 4
SWIGLU_LIMIT = 7.0
SWIGLU_ALPHA = 1.702
LANES = 128

STAGE_TILE = 512
TOKEN_TILE = 256
ROW_TILE = 752
RUN_WINDOW = 16
LOCAL_ROWS = N_EXPERTS * RUN_WINDOW + TOKEN_TILE * TOP_K
ZERO_ROWS = 16
VMEM_LIMIT = 56 * 1024 * 1024


def _rms(xf, g):
    return xf * lax.rsqrt(jnp.mean(xf * xf, axis=-1, keepdims=True) + EPS) * g


def _dot(a, b):
    return jnp.dot(a, b, preferred_element_type=F32)


ROUTE_ROWS = 4 * TOP_K
WEIGHT_ROWS = 2 * TOP_K


def _route_tile(logits):
    ne, nt = logits.shape
    eid = lax.broadcasted_iota(I32, (ne, nt), 0).astype(F32)
    neg = jnp.float32(-jnp.inf)
    lg = logits
    picked = jnp.zeros((ne, nt), F32)
    vals, sels, idxs = [], [], []
    for _ in range(TOP_K):
        m = jnp.max(lg, axis=0, keepdims=True)
        idx = jnp.min(jnp.where(lg == m, eid, float(ne)), axis=0, keepdims=True)
        sel = eid == idx
        lg = jnp.where(sel, neg, lg)
        picked = picked + sel.astype(F32)
        vals.append(m)
        sels.append(sel)
        idxs.append(idx)
    ex = [jnp.exp(v - vals[0]) for v in vals]
    den = ex[0] + ex[1] + ex[2] + ex[3]

    earlier = (lax.broadcasted_iota(I32, (nt, nt), 0) < lax.broadcasted_iota(I32, (nt, nt), 1)).astype(BF16)
    before = _dot(picked.astype(BF16), earlier)
    counts = jnp.sum(picked, axis=1, keepdims=True)
    windows = jnp.floor((counts + (RUN_WINDOW - 1)) * (1.0 / RUN_WINDOW))
    lower = (lax.broadcasted_iota(I32, (ne, ne), 0) > lax.broadcasted_iota(I32, (ne, ne), 1)).astype(BF16)
    run_start = _dot(lower, jnp.broadcast_to(windows, (ne, LANES)).astype(BF16))[:, 0:1] * float(RUN_WINDOW)

    zero_row = jnp.zeros((1, nt), F32)
    ranks = [jnp.sum(jnp.where(s, before, 0.0), axis=0, keepdims=True) for s in sels]
    places = [jnp.sum(jnp.where(s, run_start + before, 0.0), axis=0, keepdims=True) for s in sels]
    packed = jnp.concatenate(places + idxs + ranks + [zero_row] * (ROUTE_ROWS - 3 * TOP_K), axis=0)
    wts = jnp.concatenate([e / den for e in ex] + [zero_row] * (WEIGHT_ROWS - TOP_K), axis=0)
    return packed.astype(I32), wts, counts


def _stage_a_kernel(n_seq, seq_rows, carry_halo, pos0, *refs):
    if carry_halo:
        (x_ref, nm_ref, win_ref, lng_ref, lnb_ref, wsp_ref, bsp_ref, wpool_ref, pscale_ref,
         wa_ref, wb_ref, wout_ref, nmoe_ref, wr_ref, br_ref,
         h1_ref, xn2_ref, route_ref, rw_ref, cnt_ref, pool_ref, vn_ref, xc_ref, lv_ref, a_ref) = refs
        cache_ref = None
    else:
        (x_ref, cache_ref, nm_ref, win_ref, lng_ref, lnb_ref, wsp_ref, bsp_ref, wpool_ref,
         pscale_ref, wa_ref, wb_ref, wout_ref, nmoe_ref, wr_ref, br_ref,
         h1_ref, xn2_ref, route_ref, rw_ref, cnt_ref, pool_ref, vn_ref, xc_ref, lv_ref, a_ref) = refs
    d = x_ref.shape[-1]
    head_dim = d // GMLP_HEADS
    group_dim = d // len(POOL_WINDOWS)
    blk = min(GMLP_BLOCK, seq_rows)
    j = pl.program_id(1) if carry_halo else 0

    x = x_ref[...]
    n = _rms(x, nm_ref[...]).astype(BF16)

    def proj(seg):
        return _dot(n, win_ref[:, seg * d:(seg + 1) * d])

    v = jax.nn.gelu(proj(1))
    mu = jnp.mean(v, axis=-1, keepdims=True)
    vc = v - mu
    vn_ref[...] = vc * lax.rsqrt(jnp.mean(vc * vc, axis=-1, keepdims=True) + EPS) * lng_ref[...] + lnb_ref[...]
    u = jax.nn.gelu(proj(0))

    ri = lax.broadcasted_iota(I32, (GMLP_BLOCK, GMLP_BLOCK), 0) // CHUNK
    ci = lax.broadcasted_iota(I32, (GMLP_BLOCK, GMLP_BLOCK), 1) // CHUNK
    causal = (ri >= ci).astype(F32)
    for h in range(GMLP_HEADS):
        wm = (wsp_ref[h] * causal)[:blk, :blk].astype(BF16)
        bias = bsp_ref[:blk, h:h + 1]
        cols = slice(h * head_dim, (h + 1) * head_dim)
        for r0 in range(0, n_seq * seq_rows, blk):
            rows = slice(r0, r0 + blk)
            sg = _dot(wm, vn_ref[rows, cols].astype(BF16)) + bias
            a_ref[rows, cols] = (u[rows, cols] * sg).astype(BF16)
    y_a = _dot(a_ref[...], wa_ref[...])

    pb = proj(2)
    if carry_halo:
        @pl.when(j == 0)
        def _():
            xc_ref[0, 0:HALO, :] = jnp.zeros((HALO, d), F32)
    else:
        for s in range(n_seq):
            xc_ref[s, 0:HALO, :] = jnp.zeros((HALO, d), F32)
            xc_ref[s, HALO - POOL_STATE:HALO, :] = cache_ref[s]
    for s in range(n_seq):
        xc_ref[s, HALO:HALO + seq_rows, :] = pb[s * seq_rows:(s + 1) * seq_rows]

    end = HALO + seq_rows
    window_sums = [[] for _ in POOL_WINDOWS]
    for s in range(n_seq):
        src, shift = xc_ref, 1
        for lvl in range(len(POOL_WINDOWS)):
            lo = SUBLANES * (lvl + 1)
            cols = slice(lvl * group_dim, d)
            summed = src[s, lo:end, cols] + src[s, lo - shift:end - shift, cols]
            window_sums[lvl].append(summed[HALO - lo:, :group_dim])
            if lvl + 1 < len(POOL_WINDOWS):
                dst = lv_ref.at[lvl % 2]
                dst[s, lo:end, (lvl + 1) * group_dim:d] = summed[:, group_dim:]
                src, shift = dst, 2 * shift

    if carry_halo:
        pos = pos0 + j * seq_rows + lax.broadcasted_iota(I32, (seq_rows, 1), 0)
    z_groups = []
    for g, w in enumerate(POOL_WINDOWS):
        cols = slice(g * group_dim, (g + 1) * group_dim)
        if carry_halo:
            cnt = jnp.minimum(pos + 1, w).astype(F32)
        else:
            cnt = float(min(pos0 + 1, w))
        parts = [window_sums[g][s] / cnt - xc_ref[s, HALO:end, cols] for s in range(n_seq)]
        zg = parts[0] if n_seq == 1 else jnp.concatenate(parts, axis=0)
        z_groups.append(_dot(zg.astype(BF16), wpool_ref[g]) * pscale_ref[:, cols])
    z = jnp.concatenate(z_groups, axis=1)
    y_b = _dot(z.astype(BF16), wb_ref[...])

    for s in range(n_seq):
        tail = xc_ref[s, HALO + seq_rows - POOL_STATE:HALO + seq_rows, :]
        if carry_halo:
            @pl.when(j == pl.num_programs(1) - 1)
            def _():
                pool_ref[...] = tail
        else:
            pool_ref[s] = tail
    if carry_halo:
        xc_ref[0, 0:HALO, :] = xc_ref[0, seq_rows:seq_rows + HALO, :]

    g_a = jax.nn.sigmoid(proj(3))
    g_b = jax.nn.sigmoid(proj(4))
    merged = (g_a * y_a + g_b * y_b).astype(BF16)
    h1 = x + _dot(merged, wout_ref[...])
    h1_ref[...] = h1

    xn2 = _rms(h1, nmoe_ref[...]).astype(BF16)
    xn2_ref[...] = xn2.astype(F32).reshape(xn2_ref.shape)
    for t in range(n_seq * seq_rows // TOKEN_TILE):
        rows = slice(t * TOKEN_TILE, (t + 1) * TOKEN_TILE)
        logits = lax.dot_general(wr_ref[...], xn2[rows], (((1,), (1,)), ((), ())),
                                 preferred_element_type=F32) + br_ref[...]
        packed, wts, counts = _route_tile(logits)
        route_ref[t] = packed
        rw_ref[t] = wts
        cnt_ref[t] = jnp.broadcast_to(counts, (N_EXPERTS, LANES))


def _const_spec(shape, grid_rank):
    zeros = (0,) * len(shape)
    if grid_rank == 2:
        return pl.BlockSpec(shape, lambda b, j: zeros, pipeline_mode=pl.Buffered(1))
    return pl.BlockSpec(shape, lambda i: zeros, pipeline_mode=pl.Buffered(1))


def _stage_a_outputs(n_rows, d):
    return (jax.ShapeDtypeStruct((n_rows, d), F32),
            jax.ShapeDtypeStruct((n_rows, SUBLANES, d // SUBLANES), F32),
            jax.ShapeDtypeStruct((n_rows // TOKEN_TILE, ROUTE_ROWS, TOKEN_TILE), I32),
            jax.ShapeDtypeStruct((n_rows // TOKEN_TILE, WEIGHT_ROWS, TOKEN_TILE), F32),
            jax.ShapeDtypeStruct((n_rows // TOKEN_TILE, N_EXPERTS, LANES), F32))


def _stage_a_prompt(x_prompt, weights):
    nb, seq, d = x_prompt.shape
    tiles_per_seq = seq // STAGE_TILE
    sub = STAGE_TILE // TOKEN_TILE
    tok_map = lambda b, j: (b * tiles_per_seq + j, 0)
    tok_map3 = lambda b, j: (b * tiles_per_seq + j, 0, 0)
    return pl.pallas_call(
        functools.partial(_stage_a_kernel, 1, STAGE_TILE, True, 0),
        grid=(nb, tiles_per_seq),
        in_specs=[pl.BlockSpec((None, STAGE_TILE, d), lambda b, j: (b, j, 0))]
        + [_const_spec(w.shape, 2) for w in weights],
        out_specs=(
            pl.BlockSpec((STAGE_TILE, d), tok_map),
            pl.BlockSpec((STAGE_TILE, SUBLANES, d // SUBLANES), tok_map3),
            pl.BlockSpec((sub, ROUTE_ROWS, TOKEN_TILE), tok_map3),
            pl.BlockSpec((sub, WEIGHT_ROWS, TOKEN_TILE), tok_map3),
            pl.BlockSpec((sub, N_EXPERTS, LANES), tok_map3),
            pl.BlockSpec((None, POOL_STATE, d), lambda b, j: (b, 0, 0)),
        ),
        out_shape=_stage_a_outputs(nb * seq, d) + (jax.ShapeDtypeStruct((nb, POOL_STATE, d), F32),),
        scratch_shapes=[
            pltpu.VMEM((STAGE_TILE, d), F32),
            pltpu.VMEM((1, HALO + STAGE_TILE, d), F32),
            pltpu.VMEM((2, 1, HALO + STAGE_TILE, d), F32),
            pltpu.VMEM((STAGE_TILE, d), BF16),
        ],
        compiler_params=pltpu.CompilerParams(
            dimension_semantics=("arbitrary", "arbitrary"), vmem_limit_bytes=VMEM_LIMIT),
        name="stage_a_prompt",
    )(x_prompt, *weights)


def _stage_a_sample(x_sample, cache, past_len, weights):
    sb, sseq, d = x_sample.shape
    n_s = sb * sseq
    whole2 = lambda i: (0, 0)
    whole3 = lambda i: (0, 0, 0)
    return pl.pallas_call(
        functools.partial(_stage_a_kernel, sb, sseq, False, past_len),
        grid=(1,),
        in_specs=[pl.BlockSpec((n_s, d), whole2), pl.BlockSpec((sb, POOL_STATE, d), whole3)]
        + [_const_spec(w.shape, 1) for w in weights],
        out_specs=(
            pl.BlockSpec((n_s, d), whole2),
            pl.BlockSpec((n_s, SUBLANES, d // SUBLANES), whole3),
            pl.BlockSpec((n_s // TOKEN_TILE, ROUTE_ROWS, TOKEN_TILE), whole3),
            pl.BlockSpec((n_s // TOKEN_TILE, WEIGHT_ROWS, TOKEN_TILE), whole3),
            pl.BlockSpec((n_s // TOKEN_TILE, N_EXPERTS, LANES), whole3),
            pl.BlockSpec((sb, POOL_STATE, d), whole3),
            pl.BlockSpec((n_s, d), whole2),
        ),
        out_shape=_stage_a_outputs(n_s, d) + (jax.ShapeDtypeStruct((sb, POOL_STATE, d), F32),
                                              jax.ShapeDtypeStruct((n_s, d), F32)),
        scratch_shapes=[
            pltpu.VMEM((sb, HALO + sseq, d), F32),
            pltpu.VMEM((2, sb, HALO + sseq, d), F32),
            pltpu.VMEM((n_s, d), BF16),
        ],
        compiler_params=pltpu.CompilerParams(
            dimension_semantics=("arbitrary",), vmem_limit_bytes=VMEM_LIMIT),
        name="stage_a_sample",
    )(x_sample.reshape(n_s, d), cache, *weights)


def _for_tile_rows(fn):
    def body(g, c):
        for rr in range(SUBLANES):
            for k in range(TOP_K):
                fn(g * SUBLANES + rr, k)
        return c

    lax.fori_loop(0, TOKEN_TILE // SUBLANES, body, 0)


def _loop(lo, hi, fn):
    def body(i, c):
        fn(i)
        return c

    lax.fori_loop(lo, hi, body, 0)


def _dispatch_kernel(n_prompt_tiles, lpos_ref, run_dst_ref, run_src_ref, run_win_ref, pad_lo_ref, pad_hi_ref,
                     nt_ref, route_ref, xp_ref, xsm_ref, xs_hbm, pos_ref, lbuf, zbuf, sems):
    i = pl.program_id(0)
    last = pl.num_programs(0) - 1
    slot = i % 2

    routed = route_ref[0]
    expert = routed[TOP_K:2 * TOP_K]
    group_row = jnp.zeros(expert.shape, I32)
    for e in range(N_EXPERTS):
        group_row = jnp.where(expert == e, run_dst_ref[i * N_EXPERTS + e], group_row)
    pos_ref[0] = group_row + routed[2 * TOP_K:3 * TOP_K]

    @pl.when(i == 0)
    def _():
        lbuf[...] = jnp.zeros(lbuf.shape, F32)
        zbuf[...] = jnp.zeros(zbuf.shape, F32)
        pad_sem = sems.at[0]

        def pad_group(e):
            lo, hi = pad_lo_ref[e], pad_hi_ref[e]
            n_big = (hi - lo) // ZERO_ROWS
            mid = lo + n_big * ZERO_ROWS
            n_mid = (hi - mid) // SUBLANES
            fine = mid + n_mid * SUBLANES
            big = lambda c: pltpu.make_async_copy(zbuf, xs_hbm.at[pl.ds(lo + c * ZERO_ROWS, ZERO_ROWS)], pad_sem)
            med = lambda c: pltpu.make_async_copy(zbuf.at[pl.ds(0, SUBLANES)],
                                                  xs_hbm.at[pl.ds(mid + c * SUBLANES, SUBLANES)], pad_sem)
            one = lambda r: pltpu.make_async_copy(zbuf.at[0], xs_hbm.at[r], pad_sem)
            _loop(0, n_big, lambda c: big(c).start())
            _loop(0, n_mid, lambda c: med(c).start())
            _loop(fine, hi, lambda r: one(r).start())
            _loop(0, n_big, lambda c: big(c).wait())
            _loop(0, n_mid, lambda c: med(c).wait())
            _loop(fine, hi, lambda r: one(r).wait())

        _loop(0, N_EXPERTS, pad_group)

        n_chunks = ROW_TILE // ZERO_ROWS
        chunk = lambda c: pltpu.make_async_copy(zbuf, xs_hbm.at[pl.ds(c * ZERO_ROWS, ZERO_ROWS)], pad_sem)
        _loop(nt_ref[0] * n_chunks, (xs_hbm.shape[0] // ROW_TILE) * n_chunks, lambda c: chunk(c).start())
        _loop(nt_ref[0] * n_chunks, (xs_hbm.shape[0] // ROW_TILE) * n_chunks, lambda c: chunk(c).wait())

    def place(src_ref):
        def body(r, k):
            lbuf[slot, lpos_ref[k * TOKEN_TILE + r]] = src_ref[r]
        _for_tile_rows(body)

    @pl.when(i < n_prompt_tiles)
    def _():
        place(xp_ref)

    @pl.when(i >= n_prompt_tiles)
    def _():
        place(xsm_ref)

    def run_copies(tile, s, fn):
        def per_pair(e2):
            for queue in range(2):
                t = tile * N_EXPERTS + e2 * 2 + queue
                src0, dst0 = run_src_ref[t], run_dst_ref[t]

                def per_window(w, src0=src0, dst0=dst0, queue=queue):
                    fn(pltpu.make_async_copy(lbuf.at[s, pl.ds(src0 + w * RUN_WINDOW, RUN_WINDOW)],
                                             xs_hbm.at[pl.ds(dst0 + w * RUN_WINDOW, RUN_WINDOW)], sems.at[s]), queue)

                _loop(0, run_win_ref[t], per_window)

        _loop(0, N_EXPERTS // 2, per_pair)

    @pl.when(i > 0)
    def _():
        run_copies(i - 1, 1 - slot, lambda c, q: c.wait())

    run_copies(i, slot, lambda c, q: c.start(priority=q))

    @pl.when(i == last)
    def _():
        run_copies(i, slot, lambda c, q: c.wait())


def _dispatch(lpos_flat, route, run_dst, run_src, run_win, pad_lo, pad_hi, num_tiles, xn2_p, xn2_s, n_rows):
    row_shape = xn2_p.shape[1:]
    npt = xn2_p.shape[0] // TOKEN_TILE
    nst = xn2_s.shape[0] // TOKEN_TILE
    smem_all = lambda a: pl.BlockSpec(a.shape, lambda i: (0,), memory_space=pltpu.SMEM)
    return pl.pallas_call(
        functools.partial(_dispatch_kernel, npt),
        grid=(npt + nst,),
        in_specs=[pl.BlockSpec((TOKEN_TILE * TOP_K,), lambda i: (i,), memory_space=pltpu.SMEM),
                  smem_all(run_dst), smem_all(run_src), smem_all(run_win),
                  smem_all(pad_lo), smem_all(pad_hi), smem_all(num_tiles),
                  pl.BlockSpec((1, ROUTE_ROWS, TOKEN_TILE), lambda i: (i, 0, 0)),
                  pl.BlockSpec((TOKEN_TILE,) + row_shape, lambda i: (jnp.minimum(i, npt - 1), 0, 0)),
                  pl.BlockSpec((TOKEN_TILE,) + row_shape, lambda i: (jnp.maximum(i - npt, 0), 0, 0))],
        out_specs=(pl.BlockSpec(memory_space=pl.ANY),
                   pl.BlockSpec((1, TOP_K, TOKEN_TILE), lambda i: (i, 0, 0))),
        out_shape=(jax.ShapeDtypeStruct((n_rows,) + row_shape, F32),
                   jax.ShapeDtypeStruct((npt + nst, TOP_K, TOKEN_TILE), I32)),
        scratch_shapes=[pltpu.VMEM((2, LOCAL_ROWS) + row_shape, F32),
                        pltpu.VMEM((ZERO_ROWS,) + row_shape, F32),
                        pltpu.SemaphoreType.DMA((2,))],
        compiler_params=pltpu.CompilerParams(
            dimension_semantics=("arbitrary",), vmem_limit_bytes=VMEM_LIMIT),
        name="dispatch",
    )(lpos_flat, run_dst, run_src, run_win, pad_lo, pad_hi, num_tiles, route, xn2_p, xn2_s)


def _expert_kernel(te_ref, nt_ref, xs_ref, wup_ref, bup_ref, wdn_ref, bdn_ref, ys_ref):
    d_ff = wdn_ref.shape[0]
    used = pl.program_id(0) < nt_ref[0]

    @pl.when(used)
    def _():
        x = xs_ref[...].reshape(ROW_TILE, -1).astype(BF16)
        hid = _dot(x, wup_ref[...].astype(BF16)) + bup_ref[...]
        glu = jnp.minimum(hid[:, :d_ff], SWIGLU_LIMIT)
        lin = jnp.clip(hid[:, d_ff:], -SWIGLU_LIMIT, SWIGLU_LIMIT)
        act = glu * jax.nn.sigmoid(SWIGLU_ALPHA * glu) * (lin + 1.0)
        y = _dot(act.astype(BF16), wdn_ref[...].astype(BF16)) + bdn_ref[...]
        ys_ref[...] = y.reshape(ys_ref.shape)

    @pl.when(jnp.logical_not(used))
    def _():
        ys_ref[...] = jnp.zeros(ys_ref.shape, F32)


def _experts(tile_expert, num_tiles, xs, w_up, b_up, w_down, b_down):
    n_rows = xs.shape[0]
    row_shape = xs.shape[1:]
    n_e, d, d_up = w_up.shape
    d_ff = w_down.shape[1]
    exp_map = lambda i, te, nt: (te[i], 0, 0)
    grid_spec = pltpu.PrefetchScalarGridSpec(
        num_scalar_prefetch=2,
        grid=(n_rows // ROW_TILE,),
        in_specs=[
            pl.BlockSpec((ROW_TILE,) + row_shape, lambda i, te, nt: (jnp.minimum(i, nt[0] - 1), 0, 0)),
            pl.BlockSpec((None, d, d_up), exp_map),
            pl.BlockSpec((None, 1, d_up), exp_map),
            pl.BlockSpec((None, d_ff, d), exp_map),
            pl.BlockSpec((None, 1, d), exp_map),
        ],
        out_specs=pl.BlockSpec((ROW_TILE,) + row_shape, lambda i, te, nt: (i, 0, 0)),
    )
    return pl.pallas_call(
        _expert_kernel,
        grid_spec=grid_spec,
        out_shape=jax.ShapeDtypeStruct((n_rows,) + row_shape, F32),
        compiler_params=pltpu.CompilerParams(
            dimension_semantics=("arbitrary",), vmem_limit_bytes=VMEM_LIMIT),
        name="experts",
    )(tile_expert, num_tiles, xs, w_up, b_up.reshape(n_e, 1, d_up), w_down, b_down.reshape(n_e, 1, d))


def _combine_kernel(n_prompt_tiles, pos_ref, pos_next_ref, ys_hbm, w_ref, h1p_ref, h1s_ref, pp_ref, ps_ref,
                    nple_ref, wg_ref, wp_ref, nfin_ref, yp_ref, ysm_ref, gbuf, sems):
    i = pl.program_id(0)
    is_prompt = i < n_prompt_tiles
    slot = i % 2

    def gather(p_ref, s):
        return lambda r, k: pltpu.make_async_copy(ys_hbm.at[p_ref[k * TOKEN_TILE + r]], gbuf.at[s, k, r],
                                                  sems.at[s])

    @pl.when(i == 0)
    def _():
        copy = gather(pos_ref, 0)
        _for_tile_rows(lambda r, k: copy(r, k).start(priority=k % 2))

    @pl.when(i + 1 < pl.num_programs(0))
    def _():
        copy = gather(pos_next_ref, 1 - slot)
        _for_tile_rows(lambda r, k: copy(r, k).start(priority=k % 2))

    copy = gather(pos_ref, slot)
    _for_tile_rows(lambda r, k: copy(r, k).wait())

    d = h1p_ref.shape[-1]
    wts = jnp.transpose(w_ref[0])
    moe = wts[:, 0:1] * gbuf[slot, 0].reshape(TOKEN_TILE, d)
    for k in range(1, TOP_K):
        moe = moe + wts[:, k:k + 1] * gbuf[slot, k].reshape(TOKEN_TILE, d)
    h2 = jnp.where(is_prompt, h1p_ref[...], h1s_ref[...]) + moe
    p = jnp.where(is_prompt, pp_ref[...], ps_ref[...])
    n3 = _rms(h2, nple_ref[...]).astype(BF16)
    gate = jax.nn.sigmoid(_dot(n3, wg_ref[...]))
    h3 = h2 + gate * _dot(p.astype(BF16), wp_ref[...])
    y = _rms(h3, nfin_ref[...])

    @pl.when(is_prompt)
    def _():
        yp_ref[...] = y

    @pl.when(jnp.logical_not(is_prompt))
    def _():
        ysm_ref[...] = y


def _combine(pos_flat, ys, topw, h1_p, h1_s, p_p, p_s, nple, wg, wp, nfin):
    n_p, d = h1_p.shape
    n_s = h1_s.shape[0]
    ple = p_p.shape[1]
    npt = n_p // TOKEN_TILE
    nst = n_s // TOKEN_TILE
    p_tok = lambda i: (jnp.minimum(i, npt - 1), 0)
    s_tok = lambda i: (jnp.maximum(i - npt, 0), 0)
    const = lambda i: (0, 0)
    last = npt + nst - 1
    this_tile = pl.BlockSpec((TOKEN_TILE * TOP_K,), lambda i: (i,), memory_space=pltpu.SMEM)
    next_tile = pl.BlockSpec((TOKEN_TILE * TOP_K,), lambda i: (jnp.minimum(i + 1, last),), memory_space=pltpu.SMEM)
    return pl.pallas_call(
        functools.partial(_combine_kernel, npt),
        grid=(npt + nst,),
        in_specs=[
            this_tile, next_tile,
            pl.BlockSpec(memory_space=pl.ANY),
            pl.BlockSpec((1, WEIGHT_ROWS, TOKEN_TILE), lambda i: (i, 0, 0)),
            pl.BlockSpec((TOKEN_TILE, d), p_tok),
            pl.BlockSpec((TOKEN_TILE, d), s_tok),
            pl.BlockSpec((TOKEN_TILE, ple), p_tok),
            pl.BlockSpec((TOKEN_TILE, ple), s_tok),
            pl.BlockSpec((1, d), const),
            pl.BlockSpec((d, d), const),
            pl.BlockSpec((ple, d), const),
            pl.BlockSpec((1, d), const),
        ],
        out_specs=(pl.BlockSpec((TOKEN_TILE, d), p_tok), pl.BlockSpec((TOKEN_TILE, d), s_tok)),
        out_shape=(jax.ShapeDtypeStruct((n_p, d), F32), jax.ShapeDtypeStruct((n_s, d), F32)),
        scratch_shapes=[pltpu.VMEM((2, TOP_K, TOKEN_TILE) + ys.shape[1:], F32), pltpu.SemaphoreType.DMA((2,))],
        compiler_params=pltpu.CompilerParams(
            dimension_semantics=("arbitrary",), vmem_limit_bytes=VMEM_LIMIT),
        name="combine",
    )(pos_flat, pos_flat, ys, topw, h1_p, h1_s, p_p, p_s, nple, wg, wp, nfin)


def kernel(x_prompt, x_sample, cache_pool, p_prompt, p_sample, norm_mix, w_in, ln_v_g, ln_v_b, w_spatial, b_spatial, w_pool_group, pool_scale, w_branch_a, w_branch_b, w_out, norm_moe, w_router, b_router, w_up, b_up, w_down, b_down, norm_ple, w_ple_gate, w_ple_proj, norm_final):
    assert norm_mix.shape[0] == 1, "single-layer stack"
    nb, seq, d = x_prompt.shape
    sb, sseq, _ = x_sample.shape
    n_prompt = nb * seq
    n_sample = sb * sseq
    n_tok = n_prompt + n_sample
    past_len = 4096
    row = lambda v: v.reshape(1, -1)

    wr = jnp.transpose(w_router[0]).astype(BF16)
    br = b_router[0].reshape(N_EXPERTS, 1)
    weights_a = (row(norm_mix[0]), w_in[0].astype(BF16), row(ln_v_g[0]), row(ln_v_b[0]), w_spatial[0],
                 jnp.transpose(b_spatial[0]), w_pool_group[0].astype(BF16), row(pool_scale[0]),
                 w_branch_a[0].astype(BF16), w_branch_b[0].astype(BF16), w_out[0].astype(BF16),
                 row(norm_moe[0]), wr, br)
    h1_p, xn2_p, route_p, rw_p, cnt_p, pool_p = _stage_a_prompt(x_prompt, weights_a)
    h1_s, xn2_s, route_s, rw_s, cnt_s, pool_s, vn_s = _stage_a_sample(x_sample, cache_pool[0], past_len, weights_a)

    n_tab = jnp.concatenate([cnt_p[:, :, 0], cnt_s[:, :, 0]], axis=0).astype(I32)
    ahead = jnp.cumsum(n_tab, axis=0) - n_tab
    total = jnp.sum(n_tab, axis=0)
    tiles_e = (total + (RUN_WINDOW - 1) + (ROW_TILE - 1)) // ROW_TILE
    ends = jnp.cumsum(tiles_e)
    first = (ends - tiles_e) * ROW_TILE
    num_tiles = ends[-1:]
    max_tiles = (n_tok * TOP_K + N_EXPERTS * (RUN_WINDOW - 1)) // ROW_TILE + N_EXPERTS
    tile_ids = jnp.minimum(jnp.arange(max_tiles, dtype=I32), num_tiles[0] - 1)
    tile_expert = jnp.sum((tile_ids[:, None] >= ends[None, :]).astype(I32), axis=1)
    run_dst = (first[None, :] + ahead).reshape(-1)
    run_win = (n_tab + (RUN_WINDOW - 1)) // RUN_WINDOW
    run_src = ((jnp.cumsum(run_win, axis=1) - run_win) * RUN_WINDOW).reshape(-1)
    pad_lo = first + total
    pad_hi = ends * ROW_TILE

    route = jnp.concatenate([route_p, route_s], axis=0)
    topw = jnp.concatenate([rw_p, rw_s], axis=0)
    lpos_flat = route[:, :TOP_K].reshape(-1)

    xs, pos = _dispatch(lpos_flat, route, run_dst, run_src, run_win.reshape(-1), pad_lo, pad_hi, num_tiles,
                        xn2_p, xn2_s, max_tiles * ROW_TILE)
    ys = _experts(tile_expert, num_tiles, xs, w_up[0], b_up[0], w_down[0], b_down[0])

    y_p, y_s = _combine(pos.reshape(-1), ys, topw, h1_p, h1_s, p_prompt[0].reshape(n_prompt, -1),
                        p_sample[0].reshape(n_sample, -1), row(norm_ple[0]), w_ple_gate[0].astype(BF16),
                        w_ple_proj[0].astype(BF16), row(norm_final))

    return (y_p.reshape(nb, seq, d), y_s.reshape(sb, sseq, d), pool_p[None], pool_s[None],
            vn_s.reshape(1, sb, sseq, d))
```

```python
import functools

import jax
import jax.numpy as jnp
from jax import lax
from jax.experimental import pallas as pl
from jax.experimental.pallas import tpu as pltpu

F32 = jnp.float32
BF16 = jnp.bfloat16
I32 = jnp.int32

EPS = 1e-6
CHUNK = 64
GMLP_BLOCK = 128
GMLP_HEADS = 4
POOL_WINDOWS = (2, 4, 8, 16)
POOL_STATE = 15
SUBLANES = 8
HALO = SUBLANES * len(POOL_WINDOWS)
N_EXPERTS = 32
TOP_K = 4
SWIGLU_LIMIT = 7.0
SWIGLU_ALPHA = 1.702
LANES = 128

STAGE_TILE = 512
TOKEN_TILE = 256
ROW_TILE = 752
RUN_WINDOW = 16
LOCAL_ROWS = N_EXPERTS * RUN_WINDOW + TOKEN_TILE * TOP_K
ZERO_ROWS = 16
VMEM_LIMIT = 56 * 1024 * 1024


def _rms(xf, g):
    return xf * lax.rsqrt(jnp.mean(xf * xf, axis=-1, keepdims=True) + EPS) * g


def _dot(a, b):
    return jnp.dot(a, b, preferred_element_type=F32)


ROUTE_ROWS = 4 * TOP_K
WEIGHT_ROWS = 2 * TOP_K


def _route_tile(logits):
    ne, nt = logits.shape
    eid = lax.broadcasted_iota(I32, (ne, nt), 0).astype(F32)
    neg = jnp.float32(-jnp.inf)
    lg = logits
    picked = jnp.zeros((ne, nt), F32)
    vals, sels, idxs = [], [], []
    for _ in range(TOP_K):
        m = jnp.max(lg, axis=0, keepdims=True)
        idx = jnp.min(jnp.where(lg == m, eid, float(ne)), axis=0, keepdims=True)
        sel = eid == idx
        lg = jnp.where(sel, neg, lg)
        picked = picked + sel.astype(F32)
        vals.append(m)
        sels.append(sel)
        idxs.append(idx)
    ex = [jnp.exp(v - vals[0]) for v in vals]
    den = ex[0] + ex[1] + ex[2] + ex[3]

    earlier = (lax.broadcasted_iota(I32, (nt, nt), 0) < lax.broadcasted_iota(I32, (nt, nt), 1)).astype(BF16)
    before = _dot(picked.astype(BF16), earlier)
    counts = jnp.sum(picked, axis=1, keepdims=True)
    windows = jnp.floor((counts + (RUN_WINDOW - 1)) * (1.0 / RUN_WINDOW))
    lower = (lax.broadcasted_iota(I32, (ne, ne), 0) > lax.broadcasted_iota(I32, (ne, ne), 1)).astype(BF16)
    run_start = _dot(lower, jnp.broadcast_to(windows, (ne, LANES)).astype(BF16))[:, 0:1] * float(RUN_WINDOW)

    zero_row = jnp.zeros((1, nt), F32)
    ranks = [jnp.sum(jnp.where(s, before, 0.0), axis=0, keepdims=True) for s in sels]
    places = [jnp.sum(jnp.where(s, run_start + before, 0.0), axis=0, keepdims=True) for s in sels]
    packed = jnp.concatenate(places + idxs + ranks + [zero_row] * (ROUTE_ROWS - 3 * TOP_K), axis=0)
    wts = jnp.concatenate([e / den for e in ex] + [zero_row] * (WEIGHT_ROWS - TOP_K), axis=0)
    return packed.astype(I32), wts, counts


def _stage_a_kernel(n_seq, seq_rows, carry_halo, pos0, *refs):
    if carry_halo:
        (x_ref, nm_ref, win_ref, lng_ref, lnb_ref, wsp_ref, bsp_ref, wpool_ref, pscale_ref,
         wa_ref, wb_ref, wout_ref, nmoe_ref, wr_ref, br_ref,
         h1_ref, xn2_ref, route_ref, rw_ref, cnt_ref, pool_ref, vn_ref, xc_ref, lv_ref, a_ref) = refs
        cache_ref = None
    else:
        (x_ref, cache_ref, nm_ref, win_ref, lng_ref, lnb_ref, wsp_ref, bsp_ref, wpool_ref,
         pscale_ref, wa_ref, wb_ref, wout_ref, nmoe_ref, wr_ref, br_ref,
         h1_ref, xn2_ref, route_ref, rw_ref, cnt_ref, pool_ref, vn_ref, xc_ref, lv_ref, a_ref) = refs
    d = x_ref.shape[-1]
    head_dim = d // GMLP_HEADS
    group_dim = d // len(POOL_WINDOWS)
    blk = min(GMLP_BLOCK, seq_rows)
    j = pl.program_id(1) if carry_halo else 0

    x = x_ref[...]
    n = _rms(x, nm_ref[...]).astype(BF16)

    def proj(seg):
        return _dot(n, win_ref[:, seg * d:(seg + 1) * d])

    v = jax.nn.gelu(proj(1))
    mu = jnp.mean(v, axis=-1, keepdims=True)
    vc = v - mu
    vn_ref[...] = vc * lax.rsqrt(jnp.mean(vc * vc, axis=-1, keepdims=True) + EPS) * lng_ref[...] + lnb_ref[...]
    u = jax.nn.gelu(proj(0))

    ri = lax.broadcasted_iota(I32, (GMLP_BLOCK, GMLP_BLOCK), 0) // CHUNK
    ci = lax.broadcasted_iota(I32, (GMLP_BLOCK, GMLP_BLOCK), 1) // CHUNK
    causal = (ri >= ci).astype(F32)
    for h in range(GMLP_HEADS):
        wm = (wsp_ref[h] * causal)[:blk, :blk].astype(BF16)
        bias = bsp_ref[:blk, h:h + 1]
        cols = slice(h * head_dim, (h + 1) * head_dim)
        for r0 in range(0, n_seq * seq_rows, blk):
            rows = slice(r0, r0 + blk)
            sg = _dot(wm, vn_ref[rows, cols].astype(BF16)) + bias
            a_ref[rows, cols] = (u[rows, cols] * sg).astype(BF16)
    y_a = _dot(a_ref[...], wa_ref[...])

    pb = proj(2)
    if carry_halo:
        @pl.when(j == 0)
        def _():
            xc_ref[0, 0:HALO, :] = jnp.zeros((HALO, d), F32)
    else:
        for s in range(n_seq):
            xc_ref[s, 0:HALO, :] = jnp.zeros((HALO, d), F32)
            xc_ref[s, HALO - POOL_STATE:HALO, :] = cache_ref[s]
    for s in range(n_seq):
        xc_ref[s, HALO:HALO + seq_rows, :] = pb[s * seq_rows:(s + 1) * seq_rows]

    end = HALO + seq_rows
    window_sums = [[] for _ in POOL_WINDOWS]
    for s in range(n_seq):
        src, shift = xc_ref, 1
        for lvl in range(len(POOL_WINDOWS)):
            lo = SUBLANES * (lvl + 1)
            cols = slice(lvl * group_dim, d)
            summed = src[s, lo:end, cols] + src[s, lo - shift:end - shift, cols]
            window_sums[lvl].append(summed[HALO - lo:, :group_dim])
            if lvl + 1 < len(POOL_WINDOWS):
                dst = lv_ref.at[lvl % 2]
                dst[s, lo:end, (lvl + 1) * group_dim:d] = summed[:, group_dim:]
                src, shift = dst, 2 * shift

    if carry_halo:
        pos = pos0 + j * seq_rows + lax.broadcasted_iota(I32, (seq_rows, 1), 0)
    z_groups = []
    for g, w in enumerate(POOL_WINDOWS):
        cols = slice(g * group_dim, (g + 1) * group_dim)
        if carry_halo:
            cnt = jnp.minimum(pos + 1, w).astype(F32)
        else:
            cnt = float(min(pos0 + 1, w))
        parts = [window_sums[g][s] / cnt - xc_ref[s, HALO:end, cols] for s in range(n_seq)]
        zg = parts[0] if n_seq == 1 else jnp.concatenate(parts, axis=0)
        z_groups.append(_dot(zg.astype(BF16), wpool_ref[g]) * pscale_ref[:, cols])
    z = jnp.concatenate(z_groups, axis=1)
    y_b = _dot(z.astype(BF16), wb_ref[...])

    for s in range(n_seq):
        tail = xc_ref[s, HALO + seq_rows - POOL_STATE:HALO + seq_rows, :]
        if carry_halo:
            @pl.when(j == pl.num_programs(1) - 1)
            def _():
                pool_ref[...] = tail
        else:
            pool_ref[s] = tail
    if carry_halo:
        xc_ref[0, 0:HALO, :] = xc_ref[0, seq_rows:seq_rows + HALO, :]

    g_a = jax.nn.sigmoid(proj(3))
    g_b = jax.nn.sigmoid(proj(4))
    merged = (g_a * y_a + g_b * y_b).astype(BF16)
    h1 = x + _dot(merged, wout_ref[...])
    h1_ref[...] = h1

    xn2 = _rms(h1, nmoe_ref[...]).astype(BF16)
    xn2_ref[...] = xn2.astype(F32).reshape(xn2_ref.shape)
    for t in range(n_seq * seq_rows // TOKEN_TILE):
        rows = slice(t * TOKEN_TILE, (t + 1) * TOKEN_TILE)
        logits = lax.dot_general(wr_ref[...], xn2[rows], (((1,), (1,)), ((), ())),
                                 preferred_element_type=F32) + br_ref[...]
        packed, wts, counts = _route_tile(logits)
        route_ref[t] = packed
        rw_ref[t] = wts
        cnt_ref[t] = jnp.broadcast_to(counts, (N_EXPERTS, LANES))


def _const_spec(shape, grid_rank):
    zeros = (0,) * len(shape)
    if grid_rank == 2:
        return pl.BlockSpec(shape, lambda b, j: zeros, pipeline_mode=pl.Buffered(1))
    return pl.BlockSpec(shape, lambda i: zeros, pipeline_mode=pl.Buffered(1))


def _stage_a_outputs(n_rows, d):
    return (jax.ShapeDtypeStruct((n_rows, d), F32),
            jax.ShapeDtypeStruct((n_rows, SUBLANES, d // SUBLANES), F32),
            jax.ShapeDtypeStruct((n_rows // TOKEN_TILE, ROUTE_ROWS, TOKEN_TILE), I32),
            jax.ShapeDtypeStruct((n_rows // TOKEN_TILE, WEIGHT_ROWS, TOKEN_TILE), F32),
            jax.ShapeDtypeStruct((n_rows // TOKEN_TILE, N_EXPERTS, LANES), F32))


def _stage_a_prompt(x_prompt, weights):
    nb, seq, d = x_prompt.shape
    tiles_per_seq = seq // STAGE_TILE
    sub = STAGE_TILE // TOKEN_TILE
    tok_map = lambda b, j: (b * tiles_per_seq + j, 0)
    tok_map3 = lambda b, j: (b * tiles_per_seq + j, 0, 0)
    return pl.pallas_call(
        functools.partial(_stage_a_kernel, 1, STAGE_TILE, True, 0),
        grid=(nb, tiles_per_seq),
        in_specs=[pl.BlockSpec((None, STAGE_TILE, d), lambda b, j: (b, j, 0))]
        + [_const_spec(w.shape, 2) for w in weights],
        out_specs=(
            pl.BlockSpec((STAGE_TILE, d), tok_map),
            pl.BlockSpec((STAGE_TILE, SUBLANES, d // SUBLANES), tok_map3),
            pl.BlockSpec((sub, ROUTE_ROWS, TOKEN_TILE), tok_map3),
            pl.BlockSpec((sub, WEIGHT_ROWS, TOKEN_TILE), tok_map3),
            pl.BlockSpec((sub, N_EXPERTS, LANES), tok_map3),
            pl.BlockSpec((None, POOL_STATE, d), lambda b, j: (b, 0, 0)),
        ),
        out_shape=_stage_a_outputs(nb * seq, d) + (jax.ShapeDtypeStruct((nb, POOL_STATE, d), F32),),
        scratch_shapes=[
            pltpu.VMEM((STAGE_TILE, d), F32),
            pltpu.VMEM((1, HALO + STAGE_TILE, d), F32),
            pltpu.VMEM((2, 1, HALO + STAGE_TILE, d), F32),
            pltpu.VMEM((STAGE_TILE, d), BF16),
        ],
        compiler_params=pltpu.CompilerParams(
            dimension_semantics=("arbitrary", "arbitrary"), vmem_limit_bytes=VMEM_LIMIT),
        name="stage_a_prompt",
    )(x_prompt, *weights)


def _stage_a_sample(x_sample, cache, past_len, weights):
    sb, sseq, d = x_sample.shape
    n_s = sb * sseq
    whole2 = lambda i: (0, 0)
    whole3 = lambda i: (0, 0, 0)
    return pl.pallas_call(
        functools.partial(_stage_a_kernel, sb, sseq, False, past_len),
        grid=(1,),
        in_specs=[pl.BlockSpec((n_s, d), whole2), pl.BlockSpec((sb, POOL_STATE, d), whole3)]
        + [_const_spec(w.shape, 1) for w in weights],
        out_specs=(
            pl.BlockSpec((n_s, d), whole2),
            pl.BlockSpec((n_s, SUBLANES, d // SUBLANES), whole3),
            pl.BlockSpec((n_s // TOKEN_TILE, ROUTE_ROWS, TOKEN_TILE), whole3),
            pl.BlockSpec((n_s // TOKEN_TILE, WEIGHT_ROWS, TOKEN_TILE), whole3),
            pl.BlockSpec((n_s // TOKEN_TILE, N_EXPERTS, LANES), whole3),
            pl.BlockSpec((sb, POOL_STATE, d), whole3),
            pl.BlockSpec((n_s, d), whole2),
        ),
        out_shape=_stage_a_outputs(n_s, d) + (jax.ShapeDtypeStruct((sb, POOL_STATE, d), F32),
                                              jax.ShapeDtypeStruct((n_s, d), F32)),
        scratch_shapes=[
            pltpu.VMEM((sb, HALO + sseq, d), F32),
            pltpu.VMEM((2, sb, HALO + sseq, d), F32),
            pltpu.VMEM((n_s, d), BF16),
        ],
        compiler_params=pltpu.CompilerParams(
            dimension_semantics=("arbitrary",), vmem_limit_bytes=VMEM_LIMIT),
        name="stage_a_sample",
    )(x_sample.reshape(n_s, d), cache, *weights)


def _for_tile_rows(fn):
    def body(g, c):
        for rr in range(SUBLANES):
            for k in range(TOP_K):
                fn(g * SUBLANES + rr, k)
        return c

    lax.fori_loop(0, TOKEN_TILE // SUBLANES, body, 0)


def _loop(lo, hi, fn):
    def body(i, c):
        fn(i)
        return c

    lax.fori_loop(lo, hi, body, 0)


def _dispatch_kernel(n_prompt_tiles, lpos_ref, run_dst_ref, run_src_ref, run_win_ref, pad_lo_ref, pad_hi_ref,
                     nt_ref, route_ref, xp_ref, xsm_ref, xs_hbm, pos_ref, lbuf, zbuf, sems):
    i = pl.program_id(0)
    last = pl.num_programs(0) - 1
    slot = i % 2

    routed = route_ref[0]
    expert = routed[TOP_K:2 * TOP_K]
    group_row = jnp.zeros(expert.shape, I32)
    for e in range(N_EXPERTS):
        group_row = jnp.where(expert == e, run_dst_ref[i * N_EXPERTS + e], group_row)
    pos_ref[0] = group_row + routed[2 * TOP_K:3 * TOP_K]

    @pl.when(i == 0)
    def _():
        lbuf[...] = jnp.zeros(lbuf.shape, F32)
        zbuf[...] = jnp.zeros(zbuf.shape, F32)
        pad_sem = sems.at[0]

        def pad_group(e):
            lo, hi = pad_lo_ref[e], pad_hi_ref[e]
            n_big = (hi - lo) // ZERO_ROWS
            mid = lo + n_big * ZERO_ROWS
            n_mid = (hi - mid) // SUBLANES
            fine = mid + n_mid * SUBLANES
            big = lambda c: pltpu.make_async_copy(zbuf, xs_hbm.at[pl.ds(lo + c * ZERO_ROWS, ZERO_ROWS)], pad_sem)
            med = lambda c: pltpu.make_async_copy(zbuf.at[pl.ds(0, SUBLANES)],
                                                  xs_hbm.at[pl.ds(mid + c * SUBLANES, SUBLANES)], pad_sem)
            one = lambda r: pltpu.make_async_copy(zbuf.at[0], xs_hbm.at[r], pad_sem)
            _loop(0, n_big, lambda c: big(c).start())
            _loop(0, n_mid, lambda c: med(c).start())
            _loop(fine, hi, lambda r: one(r).start())
            _loop(0, n_big, lambda c: big(c).wait())
            _loop(0, n_mid, lambda c: med(c).wait())
            _loop(fine, hi, lambda r: one(r).wait())

        _loop(0, N_EXPERTS, pad_group)

        n_chunks = ROW_TILE // ZERO_ROWS
        chunk = lambda c: pltpu.make_async_copy(zbuf, xs_hbm.at[pl.ds(c * ZERO_ROWS, ZERO_ROWS)], pad_sem)
        _loop(nt_ref[0] * n_chunks, (xs_hbm.shape[0] // ROW_TILE) * n_chunks, lambda c: chunk(c).start())
        _loop(nt_ref[0] * n_chunks, (xs_hbm.shape[0] // ROW_TILE) * n_chunks, lambda c: chunk(c).wait())

    def place(src_ref):
        def body(r, k):
            lbuf[slot, lpos_ref[k * TOKEN_TILE + r]] = src_ref[r]
        _for_tile_rows(body)

    @pl.when(i < n_prompt_tiles)
    def _():
        place(xp_ref)

    @pl.when(i >= n_prompt_tiles)
    def _():
        place(xsm_ref)

    def run_copies(tile, s, fn):
        def per_pair(e2):
            for queue in range(2):
                t = tile * N_EXPERTS + e2 * 2 + queue
                src0, dst0 = run_src_ref[t], run_dst_ref[t]

                def per_window(w, src0=src0, dst0=dst0, queue=queue):
                    fn(pltpu.make_async_copy(lbuf.at[s, pl.ds(src0 + w * RUN_WINDOW, RUN_WINDOW)],
                                             xs_hbm.at[pl.ds(dst0 + w * RUN_WINDOW, RUN_WINDOW)], sems.at[s]), queue)

                _loop(0, run_win_ref[t], per_window)

        _loop(0, N_EXPERTS // 2, per_pair)

    @pl.when(i > 0)
    def _():
        run_copies(i - 1, 1 - slot, lambda c, q: c.wait())

    run_copies(i, slot, lambda c, q: c.start(priority=q))

    @pl.when(i == last)
    def _():
        run_copies(i, slot, lambda c, q: c.wait())


def _dispatch(lpos_flat, route, run_dst, run_src, run_win, pad_lo, pad_hi, num_tiles, xn2_p, xn2_s, n_rows):
    row_shape = xn2_p.shape[1:]
    npt = xn2_p.shape[0] // TOKEN_TILE
    nst = xn2_s.shape[0] // TOKEN_TILE
    smem_all = lambda a: pl.BlockSpec(a.shape, lambda i: (0,), memory_space=pltpu.SMEM)
    return pl.pallas_call(
        functools.partial(_dispatch_kernel, npt),
        grid=(npt + nst,),
        in_specs=[pl.BlockSpec((TOKEN_TILE * TOP_K,), lambda i: (i,), memory_space=pltpu.SMEM),
                  smem_all(run_dst), smem_all(run_src), smem_all(run_win),
                  smem_all(pad_lo), smem_all(pad_hi), smem_all(num_tiles),
                  pl.BlockSpec((1, ROUTE_ROWS, TOKEN_TILE), lambda i: (i, 0, 0)),
                  pl.BlockSpec((TOKEN_TILE,) + row_shape, lambda i: (jnp.minimum(i, npt - 1), 0, 0)),
                  pl.BlockSpec((TOKEN_TILE,) + row_shape, lambda i: (jnp.maximum(i - npt, 0), 0, 0))],
        out_specs=(pl.BlockSpec(memory_space=pl.ANY),
                   pl.BlockSpec((1, TOP_K, TOKEN_TILE), lambda i: (i, 0, 0))),
        out_shape=(jax.ShapeDtypeStruct((n_rows,) + row_shape, F32),
                   jax.ShapeDtypeStruct((npt + nst, TOP_K, TOKEN_TILE), I32)),
        scratch_shapes=[pltpu.VMEM((2, LOCAL_ROWS) + row_shape, F32),
                        pltpu.VMEM((ZERO_ROWS,) + row_shape, F32),
                        pltpu.SemaphoreType.DMA((2,))],
        compiler_params=pltpu.CompilerParams(
            dimension_semantics=("arbitrary",), vmem_limit_bytes=VMEM_LIMIT),
        name="dispatch",
    )(lpos_flat, run_dst, run_src, run_win, pad_lo, pad_hi, num_tiles, route, xn2_p, xn2_s)


def _expert_kernel(first_ref, tiles_ref, nt_ref, xs_hbm, wup_ref, bup_ref, wdn_ref, bdn_ref, ys_hbm,
                   xbuf, ybuf, xsem, ysem):
    e = pl.program_id(0)
    d_ff = wdn_ref.shape[0]
    n_used = nt_ref[0]

    def rows(g):
        return pl.ds(g * ROW_TILE, ROW_TILE)

    def x_copy(g):
        return pltpu.make_async_copy(xs_hbm.at[rows(g)], xbuf.at[g % 2], xsem.at[g % 2])

    def y_copy(g):
        return pltpu.make_async_copy(ybuf.at[g % 2], ys_hbm.at[rows(g)], ysem.at[g % 2])

    @pl.when(e == 0)
    def _():
        x_copy(0).start()

    def tile(g):
        @pl.when(g + 1 < n_used)
        def _():
            x_copy(g + 1).start()

        x_copy(g).wait()

        @pl.when(g >= 2)
        def _():
            y_copy(g - 2).wait()

        slot = g % 2
        x = xbuf[slot].reshape(ROW_TILE, -1).astype(BF16)
        hid = _dot(x, wup_ref[...].astype(BF16)) + bup_ref[...]
        glu = jnp.minimum(hid[:, :d_ff], SWIGLU_LIMIT)
        lin = jnp.clip(hid[:, d_ff:], -SWIGLU_LIMIT, SWIGLU_LIMIT)
        act = glu * jax.nn.sigmoid(SWIGLU_ALPHA * glu) * (lin + 1.0)
        y = _dot(act.astype(BF16), wdn_ref[...].astype(BF16)) + bdn_ref[...]
        ybuf[slot] = y.reshape(ybuf.shape[1:])
        y_copy(g).start()

    _loop(first_ref[e], first_ref[e] + tiles_ref[e], tile)

    @pl.when(e == pl.num_programs(0) - 1)
    def _():
        @pl.when(n_used >= 2)
        def _():
            y_copy(n_used - 2).wait()

        y_copy(n_used - 1).wait()
        ybuf[0] = jnp.zeros(ybuf.shape[1:], F32)
        total = ys_hbm.shape[0] // ROW_TILE
        clear = lambda g: pltpu.make_async_copy(ybuf.at[0], ys_hbm.at[rows(g)], ysem.at[0])
        _loop(n_used, total, lambda g: clear(g).start())
        _loop(n_used, total, lambda g: clear(g).wait())


def _experts(first_tile, tiles_e, num_tiles, xs, w_up, b_up, w_down, b_down):
    n_rows = xs.shape[0]
    row_shape = xs.shape[1:]
    n_e, d, d_up = w_up.shape
    d_ff = w_down.shape[1]
    exp_map = lambda e, *_: (e, 0, 0)
    grid_spec = pltpu.PrefetchScalarGridSpec(
        num_scalar_prefetch=3,
        grid=(n_e,),
        in_specs=[
            pl.BlockSpec(memory_space=pl.ANY),
            pl.BlockSpec((None, d, d_up), exp_map),
            pl.BlockSpec((None, 1, d_up), exp_map),
            pl.BlockSpec((None, d_ff, d), exp_map),
            pl.BlockSpec((None, 1, d), exp_map),
        ],
        out_specs=pl.BlockSpec(memory_space=pl.ANY),
        scratch_shapes=[pltpu.VMEM((2, ROW_TILE) + row_shape, F32), pltpu.VMEM((2, ROW_TILE) + row_shape, F32),
                        pltpu.SemaphoreType.DMA((2,)), pltpu.SemaphoreType.DMA((2,))],
    )
    return pl.pallas_call(
        _expert_kernel,
        grid_spec=grid_spec,
        out_shape=jax.ShapeDtypeStruct((n_rows,) + row_shape, F32),
        compiler_params=pltpu.CompilerParams(
            dimension_semantics=("arbitrary",), vmem_limit_bytes=VMEM_LIMIT),
        name="experts",
    )(first_tile, tiles_e, num_tiles, xs, w_up, b_up.reshape(n_e, 1, d_up), w_down, b_down.reshape(n_e, 1, d))


def _combine_kernel(n_prompt_tiles, pos_ref, pos_next_ref, ys_hbm, w_ref, h1p_ref, h1s_ref, pp_ref, ps_ref,
                    nple_ref, wg_ref, wp_ref, nfin_ref, yp_ref, ysm_ref, gbuf, sems):
    i = pl.program_id(0)
    is_prompt = i < n_prompt_tiles
    slot = i % 2

    def gather(p_ref, s):
        return lambda r, k: pltpu.make_async_copy(ys_hbm.at[p_ref[k * TOKEN_TILE + r]], gbuf.at[s, k, r],
                                                  sems.at[s])

    @pl.when(i == 0)
    def _():
        copy = gather(pos_ref, 0)
        _for_tile_rows(lambda r, k: copy(r, k).start(priority=k % 2))

    @pl.when(i + 1 < pl.num_programs(0))
    def _():
        copy = gather(pos_next_ref, 1 - slot)
        _for_tile_rows(lambda r, k: copy(r, k).start(priority=k % 2))

    copy = gather(pos_ref, slot)
    _for_tile_rows(lambda r, k: copy(r, k).wait())

    d = h1p_ref.shape[-1]
    wts = jnp.transpose(w_ref[0])
    moe = wts[:, 0:1] * gbuf[slot, 0].reshape(TOKEN_TILE, d)
    for k in range(1, TOP_K):
        moe = moe + wts[:, k:k + 1] * gbuf[slot, k].reshape(TOKEN_TILE, d)
    h2 = jnp.where(is_prompt, h1p_ref[...], h1s_ref[...]) + moe
    p = jnp.where(is_prompt, pp_ref[...], ps_ref[...])
    n3 = _rms(h2, nple_ref[...]).astype(BF16)
    gate = jax.nn.sigmoid(_dot(n3, wg_ref[...]))
    h3 = h2 + gate * _dot(p.astype(BF16), wp_ref[...])
    y = _rms(h3, nfin_ref[...])

    @pl.when(is_prompt)
    def _():
        yp_ref[...] = y

    @pl.when(jnp.logical_not(is_prompt))
    def _():
        ysm_ref[...] = y


def _combine(pos_flat, ys, topw, h1_p, h1_s, p_p, p_s, nple, wg, wp, nfin):
    n_p, d = h1_p.shape
    n_s = h1_s.shape[0]
    ple = p_p.shape[1]
    npt = n_p // TOKEN_TILE
    nst = n_s // TOKEN_TILE
    p_tok = lambda i: (jnp.minimum(i, npt - 1), 0)
    s_tok = lambda i: (jnp.maximum(i - npt, 0), 0)
    const = lambda i: (0, 0)
    last = npt + nst - 1
    this_tile = pl.BlockSpec((TOKEN_TILE * TOP_K,), lambda i: (i,), memory_space=pltpu.SMEM)
    next_tile = pl.BlockSpec((TOKEN_TILE * TOP_K,), lambda i: (jnp.minimum(i + 1, last),), memory_space=pltpu.SMEM)
    return pl.pallas_call(
        functools.partial(_combine_kernel, npt),
        grid=(npt + nst,),
        in_specs=[
            this_tile, next_tile,
            pl.BlockSpec(memory_space=pl.ANY),
            pl.BlockSpec((1, WEIGHT_ROWS, TOKEN_TILE), lambda i: (i, 0, 0)),
            pl.BlockSpec((TOKEN_TILE, d), p_tok),
            pl.BlockSpec((TOKEN_TILE, d), s_tok),
            pl.BlockSpec((TOKEN_TILE, ple), p_tok),
            pl.BlockSpec((TOKEN_TILE, ple), s_tok),
            pl.BlockSpec((1, d), const),
            pl.BlockSpec((d, d), const),
            pl.BlockSpec((ple, d), const),
            pl.BlockSpec((1, d), const),
        ],
        out_specs=(pl.BlockSpec((TOKEN_TILE, d), p_tok), pl.BlockSpec((TOKEN_TILE, d), s_tok)),
        out_shape=(jax.ShapeDtypeStruct((n_p, d), F32), jax.ShapeDtypeStruct((n_s, d), F32)),
        scratch_shapes=[pltpu.VMEM((2, TOP_K, TOKEN_TILE) + ys.shape[1:], F32), pltpu.SemaphoreType.DMA((2,))],
        compiler_params=pltpu.CompilerParams(
            dimension_semantics=("arbitrary",), vmem_limit_bytes=VMEM_LIMIT),
        name="combine",
    )(pos_flat, pos_flat, ys, topw, h1_p, h1_s, p_p, p_s, nple, wg, wp, nfin)


def kernel(x_prompt, x_sample, cache_pool, p_prompt, p_sample, norm_mix, w_in, ln_v_g, ln_v_b, w_spatial, b_spatial, w_pool_group, pool_scale, w_branch_a, w_branch_b, w_out, norm_moe, w_router, b_router, w_up, b_up, w_down, b_down, norm_ple, w_ple_gate, w_ple_proj, norm_final):
    assert norm_mix.shape[0] == 1, "single-layer stack"
    nb, seq, d = x_prompt.shape
    sb, sseq, _ = x_sample.shape
    n_prompt = nb * seq
    n_sample = sb * sseq
    n_tok = n_prompt + n_sample
    past_len = 4096
    row = lambda v: v.reshape(1, -1)

    wr = jnp.transpose(w_router[0]).astype(BF16)
    br = b_router[0].reshape(N_EXPERTS, 1)
    weights_a = (row(norm_mix[0]), w_in[0].astype(BF16), row(ln_v_g[0]), row(ln_v_b[0]), w_spatial[0],
                 jnp.transpose(b_spatial[0]), w_pool_group[0].astype(BF16), row(pool_scale[0]),
                 w_branch_a[0].astype(BF16), w_branch_b[0].astype(BF16), w_out[0].astype(BF16),
                 row(norm_moe[0]), wr, br)
    h1_p, xn2_p, route_p, rw_p, cnt_p, pool_p = _stage_a_prompt(x_prompt, weights_a)
    h1_s, xn2_s, route_s, rw_s, cnt_s, pool_s, vn_s = _stage_a_sample(x_sample, cache_pool[0], past_len, weights_a)

    n_tab = jnp.concatenate([cnt_p[:, :, 0], cnt_s[:, :, 0]], axis=0).astype(I32)
    ahead = jnp.cumsum(n_tab, axis=0) - n_tab
    total = jnp.sum(n_tab, axis=0)
    tiles_e = (total + (RUN_WINDOW - 1) + (ROW_TILE - 1)) // ROW_TILE
    ends = jnp.cumsum(tiles_e)
    first = (ends - tiles_e) * ROW_TILE
    num_tiles = ends[-1:]
    max_tiles = (n_tok * TOP_K + N_EXPERTS * (RUN_WINDOW - 1)) // ROW_TILE + N_EXPERTS
    run_dst = (first[None, :] + ahead).reshape(-1)
    run_win = (n_tab + (RUN_WINDOW - 1)) // RUN_WINDOW
    run_src = ((jnp.cumsum(run_win, axis=1) - run_win) * RUN_WINDOW).reshape(-1)
    pad_lo = first + total
    pad_hi = ends * ROW_TILE

    route = jnp.concatenate([route_p, route_s], axis=0)
    topw = jnp.concatenate([rw_p, rw_s], axis=0)
    lpos_flat = route[:, :TOP_K].reshape(-1)

    xs, pos = _dispatch(lpos_flat, route, run_dst, run_src, run_win.reshape(-1), pad_lo, pad_hi, num_tiles,
                        xn2_p, xn2_s, max_tiles * ROW_TILE)
    ys = _experts(ends - tiles_e, tiles_e, num_tiles, xs, w_up[0], b_up[0], w_down[0], b_down[0])

    y_p, y_s = _combine(pos.reshape(-1), ys, topw, h1_p, h1_s, p_prompt[0].reshape(n_prompt, -1),
                        p_sample[0].reshape(n_sample, -1), row(norm_ple[0]), w_ple_gate[0].astype(BF16),
                        w_ple_proj[0].astype(BF16), row(norm_final))

    return (y_p.reshape(nb, seq, d), y_s.reshape(sb, sseq, d), pool_p[None], pool_s[None],
            vn_s.reshape(1, sb, sseq, d))
```

```python
import functools

import jax
import jax.numpy as jnp
from jax import lax
from jax.experimental import pallas as pl
from jax.experimental.pallas import tpu as pltpu

F32 = jnp.float32
BF16 = jnp.bfloat16
I32 = jnp.int32

EPS = 1e-6
CHUNK = 64
GMLP_BLOCK = 128
GMLP_HEADS = 4
POOL_WINDOWS = (2, 4, 8, 16)
POOL_STATE = 15
SUBLANES = 8
HALO = SUBLANES * len(POOL_WINDOWS)
N_EXPERTS = 32
TOP_K = 4
SWIGLU_LIMIT = 7.0
SWIGLU_ALPHA = 1.702
LANES = 128

STAGE_TILE = 512
TOKEN_TILE = 256
ROW_TILE = 752
RUN_WINDOW = 16
LOCAL_ROWS = N_EXPERTS * RUN_WINDOW + TOKEN_TILE * TOP_K
VMEM_LIMIT = 56 * 1024 * 1024


def _rms(xf, g):
    return xf * lax.rsqrt(jnp.mean(xf * xf, axis=-1, keepdims=True) + EPS) * g


def _dot(a, b):
    return jnp.dot(a, b, preferred_element_type=F32)


ROUTE_ROWS = 4 * TOP_K
WEIGHT_ROWS = 2 * TOP_K


def _route_tile(logits):
    ne, nt = logits.shape
    eid = lax.broadcasted_iota(I32, (ne, nt), 0).astype(F32)
    neg = jnp.float32(-jnp.inf)
    lg = logits
    picked = jnp.zeros((ne, nt), F32)
    vals, sels, idxs = [], [], []
    for _ in range(TOP_K):
        m = jnp.max(lg, axis=0, keepdims=True)
        idx = jnp.min(jnp.where(lg == m, eid, float(ne)), axis=0, keepdims=True)
        sel = eid == idx
        lg = jnp.where(sel, neg, lg)
        picked = picked + sel.astype(F32)
        vals.append(m)
        sels.append(sel)
        idxs.append(idx)
    ex = [jnp.exp(v - vals[0]) for v in vals]
    den = ex[0] + ex[1] + ex[2] + ex[3]

    earlier = (lax.broadcasted_iota(I32, (nt, nt), 0) < lax.broadcasted_iota(I32, (nt, nt), 1)).astype(BF16)
    before = _dot(picked.astype(BF16), earlier)
    counts = jnp.sum(picked, axis=1, keepdims=True)
    windows = jnp.floor((counts + (RUN_WINDOW - 1)) * (1.0 / RUN_WINDOW))
    lower = (lax.broadcasted_iota(I32, (ne, ne), 0) > lax.broadcasted_iota(I32, (ne, ne), 1)).astype(BF16)
    run_start = _dot(lower, jnp.broadcast_to(windows, (ne, LANES)).astype(BF16))[:, 0:1] * float(RUN_WINDOW)

    zero_row = jnp.zeros((1, nt), F32)
    ranks = [jnp.sum(jnp.where(s, before, 0.0), axis=0, keepdims=True) for s in sels]
    places = [jnp.sum(jnp.where(s, run_start + before, 0.0), axis=0, keepdims=True) for s in sels]
    packed = jnp.concatenate(places + idxs + ranks + [zero_row] * (ROUTE_ROWS - 3 * TOP_K), axis=0)
    wts = jnp.concatenate([e / den for e in ex] + [zero_row] * (WEIGHT_ROWS - TOP_K), axis=0)
    return packed.astype(I32), wts, counts


def _stage_a_kernel(n_seq, seq_rows, carry_halo, pos0, *refs):
    if carry_halo:
        (x_ref, nm_ref, win_ref, lng_ref, lnb_ref, wsp_ref, bsp_ref, wpool_ref, pscale_ref,
         wa_ref, wb_ref, wout_ref, nmoe_ref, wr_ref, br_ref,
         h1_ref, xn2_ref, route_ref, rw_ref, cnt_ref, pool_ref, vn_ref, xc_ref, lv_ref, a_ref) = refs
        cache_ref = None
    else:
        (x_ref, cache_ref, nm_ref, win_ref, lng_ref, lnb_ref, wsp_ref, bsp_ref, wpool_ref,
         pscale_ref, wa_ref, wb_ref, wout_ref, nmoe_ref, wr_ref, br_ref,
         h1_ref, xn2_ref, route_ref, rw_ref, cnt_ref, pool_ref, vn_ref, xc_ref, lv_ref, a_ref) = refs
    d = x_ref.shape[-1]
    head_dim = d // GMLP_HEADS
    group_dim = d // len(POOL_WINDOWS)
    blk = min(GMLP_BLOCK, seq_rows)
    j = pl.program_id(1) if carry_halo else 0

    x = x_ref[...]
    n = _rms(x, nm_ref[...]).astype(BF16)

    def proj(seg):
        return _dot(n, win_ref[:, seg * d:(seg + 1) * d])

    v = jax.nn.gelu(proj(1))
    mu = jnp.mean(v, axis=-1, keepdims=True)
    vc = v - mu
    vn_ref[...] = vc * lax.rsqrt(jnp.mean(vc * vc, axis=-1, keepdims=True) + EPS) * lng_ref[...] + lnb_ref[...]
    u = jax.nn.gelu(proj(0))

    ri = lax.broadcasted_iota(I32, (GMLP_BLOCK, GMLP_BLOCK), 0) // CHUNK
    ci = lax.broadcasted_iota(I32, (GMLP_BLOCK, GMLP_BLOCK), 1) // CHUNK
    causal = (ri >= ci).astype(F32)
    for h in range(GMLP_HEADS):
        wm = (wsp_ref[h] * causal)[:blk, :blk].astype(BF16)
        bias = bsp_ref[:blk, h:h + 1]
        cols = slice(h * head_dim, (h + 1) * head_dim)
        for r0 in range(0, n_seq * seq_rows, blk):
            rows = slice(r0, r0 + blk)
            sg = _dot(wm, vn_ref[rows, cols].astype(BF16)) + bias
            a_ref[rows, cols] = (u[rows, cols] * sg).astype(BF16)
    y_a = _dot(a_ref[...], wa_ref[...])

    pb = proj(2)
    if carry_halo:
        @pl.when(j == 0)
        def _():
            xc_ref[0, 0:HALO, :] = jnp.zeros((HALO, d), F32)
    else:
        for s in range(n_seq):
            xc_ref[s, 0:HALO, :] = jnp.zeros((HALO, d), F32)
            xc_ref[s, HALO - POOL_STATE:HALO, :] = cache_ref[s]
    for s in range(n_seq):
        xc_ref[s, HALO:HALO + seq_rows, :] = pb[s * seq_rows:(s + 1) * seq_rows]

    end = HALO + seq_rows
    window_sums = [[] for _ in POOL_WINDOWS]
    for s in range(n_seq):
        src, shift = xc_ref, 1
        for lvl in range(len(POOL_WINDOWS)):
            lo = SUBLANES * (lvl + 1)
            cols = slice(lvl * group_dim, d)
            summed = src[s, lo:end, cols] + src[s, lo - shift:end - shift, cols]
            window_sums[lvl].append(summed[HALO - lo:, :group_dim])
            if lvl + 1 < len(POOL_WINDOWS):
                dst = lv_ref.at[lvl % 2]
                dst[s, lo:end, (lvl + 1) * group_dim:d] = summed[:, group_dim:]
                src, shift = dst, 2 * shift

    if carry_halo:
        pos = pos0 + j * seq_rows + lax.broadcasted_iota(I32, (seq_rows, 1), 0)
    z_groups = []
    for g, w in enumerate(POOL_WINDOWS):
        cols = slice(g * group_dim, (g + 1) * group_dim)
        if carry_halo:
            cnt = jnp.minimum(pos + 1, w).astype(F32)
        else:
            cnt = float(min(pos0 + 1, w))
        parts = [window_sums[g][s] / cnt - xc_ref[s, HALO:end, cols] for s in range(n_seq)]
        zg = parts[0] if n_seq == 1 else jnp.concatenate(parts, axis=0)
        z_groups.append(_dot(zg.astype(BF16), wpool_ref[g]) * pscale_ref[:, cols])
    z = jnp.concatenate(z_groups, axis=1)
    y_b = _dot(z.astype(BF16), wb_ref[...])

    for s in range(n_seq):
        tail = xc_ref[s, HALO + seq_rows - POOL_STATE:HALO + seq_rows, :]
        if carry_halo:
            @pl.when(j == pl.num_programs(1) - 1)
            def _():
                pool_ref[...] = tail
        else:
            pool_ref[s] = tail
    if carry_halo:
        xc_ref[0, 0:HALO, :] = xc_ref[0, seq_rows:seq_rows + HALO, :]

    g_a = jax.nn.sigmoid(proj(3))
    g_b = jax.nn.sigmoid(proj(4))
    merged = (g_a * y_a + g_b * y_b).astype(BF16)
    h1 = x + _dot(merged, wout_ref[...])
    h1_ref[...] = h1

    xn2 = _rms(h1, nmoe_ref[...]).astype(BF16)
    xn2_ref[...] = xn2.astype(F32).reshape(xn2_ref.shape)
    for t in range(n_seq * seq_rows // TOKEN_TILE):
        rows = slice(t * TOKEN_TILE, (t + 1) * TOKEN_TILE)
        logits = lax.dot_general(wr_ref[...], xn2[rows], (((1,), (1,)), ((), ())),
                                 preferred_element_type=F32) + br_ref[...]
        packed, wts, counts = _route_tile(logits)
        route_ref[t] = packed
        rw_ref[t] = wts
        cnt_ref[t] = jnp.broadcast_to(counts, (N_EXPERTS, LANES))


def _const_spec(shape, grid_rank):
    zeros = (0,) * len(shape)
    if grid_rank == 2:
        return pl.BlockSpec(shape, lambda b, j: zeros, pipeline_mode=pl.Buffered(1))
    return pl.BlockSpec(shape, lambda i: zeros, pipeline_mode=pl.Buffered(1))


def _stage_a_outputs(n_rows, d):
    return (jax.ShapeDtypeStruct((n_rows, d), F32),
            jax.ShapeDtypeStruct((n_rows, SUBLANES, d // SUBLANES), F32),
            jax.ShapeDtypeStruct((n_rows // TOKEN_TILE, ROUTE_ROWS, TOKEN_TILE), I32),
            jax.ShapeDtypeStruct((n_rows // TOKEN_TILE, WEIGHT_ROWS, TOKEN_TILE), F32),
            jax.ShapeDtypeStruct((n_rows // TOKEN_TILE, N_EXPERTS, LANES), F32))


def _stage_a_prompt(x_prompt, weights):
    nb, seq, d = x_prompt.shape
    tiles_per_seq = seq // STAGE_TILE
    sub = STAGE_TILE // TOKEN_TILE
    tok_map = lambda b, j: (b * tiles_per_seq + j, 0)
    tok_map3 = lambda b, j: (b * tiles_per_seq + j, 0, 0)
    return pl.pallas_call(
        functools.partial(_stage_a_kernel, 1, STAGE_TILE, True, 0),
        grid=(nb, tiles_per_seq),
        in_specs=[pl.BlockSpec((None, STAGE_TILE, d), lambda b, j: (b, j, 0))]
        + [_const_spec(w.shape, 2) for w in weights],
        out_specs=(
            pl.BlockSpec((STAGE_TILE, d), tok_map),
            pl.BlockSpec((STAGE_TILE, SUBLANES, d // SUBLANES), tok_map3),
            pl.BlockSpec((sub, ROUTE_ROWS, TOKEN_TILE), tok_map3),
            pl.BlockSpec((sub, WEIGHT_ROWS, TOKEN_TILE), tok_map3),
            pl.BlockSpec((sub, N_EXPERTS, LANES), tok_map3),
            pl.BlockSpec((None, POOL_STATE, d), lambda b, j: (b, 0, 0)),
        ),
        out_shape=_stage_a_outputs(nb * seq, d) + (jax.ShapeDtypeStruct((nb, POOL_STATE, d), F32),),
        scratch_shapes=[
            pltpu.VMEM((STAGE_TILE, d), F32),
            pltpu.VMEM((1, HALO + STAGE_TILE, d), F32),
            pltpu.VMEM((2, 1, HALO + STAGE_TILE, d), F32),
            pltpu.VMEM((STAGE_TILE, d), BF16),
        ],
        compiler_params=pltpu.CompilerParams(
            dimension_semantics=("arbitrary", "arbitrary"), vmem_limit_bytes=VMEM_LIMIT),
        name="stage_a_prompt",
    )(x_prompt, *weights)


def _stage_a_sample(x_sample, cache, past_len, weights):
    sb, sseq, d = x_sample.shape
    n_s = sb * sseq
    whole2 = lambda i: (0, 0)
    whole3 = lambda i: (0, 0, 0)
    return pl.pallas_call(
        functools.partial(_stage_a_kernel, sb, sseq, False, past_len),
        grid=(1,),
        in_specs=[pl.BlockSpec((n_s, d), whole2), pl.BlockSpec((sb, POOL_STATE, d), whole3)]
        + [_const_spec(w.shape, 1) for w in weights],
        out_specs=(
            pl.BlockSpec((n_s, d), whole2),
            pl.BlockSpec((n_s, SUBLANES, d // SUBLANES), whole3),
            pl.BlockSpec((n_s // TOKEN_TILE, ROUTE_ROWS, TOKEN_TILE), whole3),
            pl.BlockSpec((n_s // TOKEN_TILE, WEIGHT_ROWS, TOKEN_TILE), whole3),
            pl.BlockSpec((n_s // TOKEN_TILE, N_EXPERTS, LANES), whole3),
            pl.BlockSpec((sb, POOL_STATE, d), whole3),
            pl.BlockSpec((n_s, d), whole2),
        ),
        out_shape=_stage_a_outputs(n_s, d) + (jax.ShapeDtypeStruct((sb, POOL_STATE, d), F32),
                                              jax.ShapeDtypeStruct((n_s, d), F32)),
        scratch_shapes=[
            pltpu.VMEM((sb, HALO + sseq, d), F32),
            pltpu.VMEM((2, sb, HALO + sseq, d), F32),
            pltpu.VMEM((n_s, d), BF16),
        ],
        compiler_params=pltpu.CompilerParams(
            dimension_semantics=("arbitrary",), vmem_limit_bytes=VMEM_LIMIT),
        name="stage_a_sample",
    )(x_sample.reshape(n_s, d), cache, *weights)


def _for_tile_rows(fn):
    def body(g, c):
        for rr in range(SUBLANES):
            for k in range(TOP_K):
                fn(g * SUBLANES + rr, k)
        return c

    lax.fori_loop(0, TOKEN_TILE // SUBLANES, body, 0)


def _loop(lo, hi, fn):
    def body(i, c):
        fn(i)
        return c

    lax.fori_loop(lo, hi, body, 0)


def _dispatch_kernel(n_prompt_tiles, lpos_ref, run_dst_ref, run_src_ref, run_win_ref, pad_lo_ref, pad_hi_ref,
                     nt_ref, route_ref, xp_ref, xsm_ref, xs_hbm, pos_ref, lbuf, zbuf, sems):
    i = pl.program_id(0)
    last = pl.num_programs(0) - 1
    slot = i % 2

    routed = route_ref[0]
    expert = routed[TOP_K:2 * TOP_K]
    group_row = jnp.zeros(expert.shape, I32)
    for e in range(N_EXPERTS):
        group_row = jnp.where(expert == e, run_dst_ref[i * N_EXPERTS + e], group_row)
    pos_ref[0] = group_row + routed[2 * TOP_K:3 * TOP_K]

    @pl.when(i == 0)
    def _():
        lbuf[...] = jnp.zeros(lbuf.shape, F32)
        zbuf[...] = jnp.zeros(zbuf.shape, F32)
        pad_sem = sems.at[0, 0]

        def pad_group(e):
            lo, hi = pad_lo_ref[e], pad_hi_ref[e]
            n_big = (hi - lo) // RUN_WINDOW
            mid = lo + n_big * RUN_WINDOW
            n_mid = (hi - mid) // SUBLANES
            fine = mid + n_mid * SUBLANES
            big = lambda c: pltpu.make_async_copy(zbuf.at[pl.ds(0, RUN_WINDOW)],
                                                  xs_hbm.at[pl.ds(lo + c * RUN_WINDOW, RUN_WINDOW)], pad_sem)
            med = lambda c: pltpu.make_async_copy(zbuf.at[pl.ds(0, SUBLANES)],
                                                  xs_hbm.at[pl.ds(mid + c * SUBLANES, SUBLANES)], pad_sem)
            one = lambda r: pltpu.make_async_copy(zbuf.at[0], xs_hbm.at[r], pad_sem)
            _loop(0, n_big, lambda c: big(c).start())
            _loop(0, n_mid, lambda c: med(c).start())
            _loop(fine, hi, lambda r: one(r).start())
            _loop(0, n_big, lambda c: big(c).wait())
            _loop(0, n_mid, lambda c: med(c).wait())
            _loop(fine, hi, lambda r: one(r).wait())

        _loop(0, N_EXPERTS, pad_group)

        whole = lambda t: pltpu.make_async_copy(zbuf, xs_hbm.at[pl.ds(t * ROW_TILE, ROW_TILE)], pad_sem)
        _loop(nt_ref[0], xs_hbm.shape[0] // ROW_TILE, lambda t: whole(t).start())
        _loop(nt_ref[0], xs_hbm.shape[0] // ROW_TILE, lambda t: whole(t).wait())

    def place(src_ref):
        def body(r, k):
            lbuf[slot, lpos_ref[k * TOKEN_TILE + r]] = src_ref[r]
        _for_tile_rows(body)

    @pl.when(i < n_prompt_tiles)
    def _():
        place(xp_ref)

    @pl.when(i >= n_prompt_tiles)
    def _():
        place(xsm_ref)

    def run_copies(tile, s, e, fn):
        t = tile * N_EXPERTS + e
        src0, dst0 = run_src_ref[t], run_dst_ref[t]
        _loop(0, run_win_ref[t], lambda w: fn(pltpu.make_async_copy(
            lbuf.at[s, pl.ds(src0 + w * RUN_WINDOW, RUN_WINDOW)],
            xs_hbm.at[pl.ds(dst0 + w * RUN_WINDOW, RUN_WINDOW)], sems.at[s, e])))

    def per_pair(e2):
        for queue in range(2):
            e = e2 * 2 + queue

            @pl.when(i > 0)
            def _():
                run_copies(i - 1, 1 - slot, e, lambda c: c.wait())

            run_copies(i, slot, e, lambda c: c.start(priority=queue))

    _loop(0, N_EXPERTS // 2, per_pair)

    @pl.when(i == last)
    def _():
        _loop(0, N_EXPERTS, lambda e: run_copies(i, slot, e, lambda c: c.wait()))


def _dispatch(lpos_flat, route, run_dst, run_src, run_win, pad_lo, pad_hi, num_tiles, xn2_p, xn2_s, n_rows):
    row_shape = xn2_p.shape[1:]
    npt = xn2_p.shape[0] // TOKEN_TILE
    nst = xn2_s.shape[0] // TOKEN_TILE
    smem_all = lambda a: pl.BlockSpec(a.shape, lambda i: (0,), memory_space=pltpu.SMEM)
    return pl.pallas_call(
        functools.partial(_dispatch_kernel, npt),
        grid=(npt + nst,),
        in_specs=[pl.BlockSpec((TOKEN_TILE * TOP_K,), lambda i: (i,), memory_space=pltpu.SMEM),
                  smem_all(run_dst), smem_all(run_src), smem_all(run_win),
                  smem_all(pad_lo), smem_all(pad_hi), smem_all(num_tiles),
                  pl.BlockSpec((1, ROUTE_ROWS, TOKEN_TILE), lambda i: (i, 0, 0)),
                  pl.BlockSpec((TOKEN_TILE,) + row_shape, lambda i: (jnp.minimum(i, npt - 1), 0, 0)),
                  pl.BlockSpec((TOKEN_TILE,) + row_shape, lambda i: (jnp.maximum(i - npt, 0), 0, 0))],
        out_specs=(pl.BlockSpec(memory_space=pl.ANY),
                   pl.BlockSpec((1, TOP_K, TOKEN_TILE), lambda i: (i, 0, 0))),
        out_shape=(jax.ShapeDtypeStruct((n_rows,) + row_shape, F32),
                   jax.ShapeDtypeStruct((npt + nst, TOP_K, TOKEN_TILE), I32)),
        scratch_shapes=[pltpu.VMEM((2, LOCAL_ROWS) + row_shape, F32),
                        pltpu.VMEM((ROW_TILE,) + row_shape, F32),
                        pltpu.SemaphoreType.DMA((2, N_EXPERTS))],
        compiler_params=pltpu.CompilerParams(
            dimension_semantics=("arbitrary",), vmem_limit_bytes=VMEM_LIMIT),
        name="dispatch",
    )(lpos_flat, run_dst, run_src, run_win, pad_lo, pad_hi, num_tiles, route, xn2_p, xn2_s)


def _expert_kernel(te_ref, nt_ref, xs_ref, wup_ref, bup_ref, wdn_ref, bdn_ref, ys_ref):
    d_ff = wdn_ref.shape[0]
    used = pl.program_id(0) < nt_ref[0]

    @pl.when(used)
    def _():
        x = xs_ref[...].reshape(ROW_TILE, -1).astype(BF16)
        hid = _dot(x, wup_ref[...].astype(BF16)) + bup_ref[...]
        glu = jnp.minimum(hid[:, :d_ff], SWIGLU_LIMIT)
        lin = jnp.clip(hid[:, d_ff:], -SWIGLU_LIMIT, SWIGLU_LIMIT)
        act = glu * jax.nn.sigmoid(SWIGLU_ALPHA * glu) * (lin + 1.0)
        y = _dot(act.astype(BF16), wdn_ref[...].astype(BF16)) + bdn_ref[...]
        ys_ref[...] = y.reshape(ys_ref.shape)

    @pl.when(jnp.logical_not(used))
    def _():
        ys_ref[...] = jnp.zeros(ys_ref.shape, F32)


def _experts(tile_expert, num_tiles, xs, w_up, b_up, w_down, b_down):
    n_rows = xs.shape[0]
    row_shape = xs.shape[1:]
    n_e, d, d_up = w_up.shape
    d_ff = w_down.shape[1]
    exp_map = lambda i, te, nt: (te[i], 0, 0)
    grid_spec = pltpu.PrefetchScalarGridSpec(
        num_scalar_prefetch=2,
        grid=(n_rows // ROW_TILE,),
        in_specs=[
            pl.BlockSpec((ROW_TILE,) + row_shape, lambda i, te, nt: (jnp.minimum(i, nt[0] - 1), 0, 0)),
            pl.BlockSpec((None, d, d_up), exp_map),
            pl.BlockSpec((None, 1, d_up), exp_map),
            pl.BlockSpec((None, d_ff, d), exp_map),
            pl.BlockSpec((None, 1, d), exp_map),
        ],
        out_specs=pl.BlockSpec((ROW_TILE,) + row_shape, lambda i, te, nt: (i, 0, 0)),
    )
    return pl.pallas_call(
        _expert_kernel,
        grid_spec=grid_spec,
        out_shape=jax.ShapeDtypeStruct((n_rows,) + row_shape, F32),
        compiler_params=pltpu.CompilerParams(
            dimension_semantics=("arbitrary",), vmem_limit_bytes=VMEM_LIMIT),
        name="experts",
    )(tile_expert, num_tiles, xs, w_up, b_up.reshape(n_e, 1, d_up), w_down, b_down.reshape(n_e, 1, d))


def _combine_kernel(n_prompt_tiles, pos_ref, pos_next_ref, ys_hbm, w_ref, h1p_ref, h1s_ref, pp_ref, ps_ref,
                    nple_ref, wg_ref, wp_ref, nfin_ref, yp_ref, ysm_ref, gbuf, sems):
    i = pl.program_id(0)
    is_prompt = i < n_prompt_tiles
    slot = i % 2

    def gather(p_ref, s):
        return lambda r, k: pltpu.make_async_copy(ys_hbm.at[p_ref[k * TOKEN_TILE + r]], gbuf.at[s, k, r],
                                                  sems.at[s])

    @pl.when(i == 0)
    def _():
        copy = gather(pos_ref, 0)
        _for_tile_rows(lambda r, k: copy(r, k).start(priority=k % 2))

    @pl.when(i + 1 < pl.num_programs(0))
    def _():
        copy = gather(pos_next_ref, 1 - slot)
        _for_tile_rows(lambda r, k: copy(r, k).start(priority=k % 2))

    copy = gather(pos_ref, slot)
    _for_tile_rows(lambda r, k: copy(r, k).wait())

    d = h1p_ref.shape[-1]
    wts = jnp.transpose(w_ref[0])
    moe = wts[:, 0:1] * gbuf[slot, 0].reshape(TOKEN_TILE, d)
    for k in range(1, TOP_K):
        moe = moe + wts[:, k:k + 1] * gbuf[slot, k].reshape(TOKEN_TILE, d)
    h2 = jnp.where(is_prompt, h1p_ref[...], h1s_ref[...]) + moe
    p = jnp.where(is_prompt, pp_ref[...], ps_ref[...])
    n3 = _rms(h2, nple_ref[...]).astype(BF16)
    gate = jax.nn.sigmoid(_dot(n3, wg_ref[...]))
    h3 = h2 + gate * _dot(p.astype(BF16), wp_ref[...])
    y = _rms(h3, nfin_ref[...])

    @pl.when(is_prompt)
    def _():
        yp_ref[...] = y

    @pl.when(jnp.logical_not(is_prompt))
    def _():
        ysm_ref[...] = y


def _combine(pos_flat, ys, topw, h1_p, h1_s, p_p, p_s, nple, wg, wp, nfin):
    n_p, d = h1_p.shape
    n_s = h1_s.shape[0]
    ple = p_p.shape[1]
    npt = n_p // TOKEN_TILE
    nst = n_s // TOKEN_TILE
    p_tok = lambda i: (jnp.minimum(i, npt - 1), 0)
    s_tok = lambda i: (jnp.maximum(i - npt, 0), 0)
    const = lambda i: (0, 0)
    last = npt + nst - 1
    this_tile = pl.BlockSpec((TOKEN_TILE * TOP_K,), lambda i: (i,), memory_space=pltpu.SMEM)
    next_tile = pl.BlockSpec((TOKEN_TILE * TOP_K,), lambda i: (jnp.minimum(i + 1, last),), memory_space=pltpu.SMEM)
    return pl.pallas_call(
        functools.partial(_combine_kernel, npt),
        grid=(npt + nst,),
        in_specs=[
            this_tile, next_tile,
            pl.BlockSpec(memory_space=pl.ANY),
            pl.BlockSpec((1, WEIGHT_ROWS, TOKEN_TILE), lambda i: (i, 0, 0)),
            pl.BlockSpec((TOKEN_TILE, d), p_tok),
            pl.BlockSpec((TOKEN_TILE, d), s_tok),
            pl.BlockSpec((TOKEN_TILE, ple), p_tok),
            pl.BlockSpec((TOKEN_TILE, ple), s_tok),
            pl.BlockSpec((1, d), const),
            pl.BlockSpec((d, d), const),
            pl.BlockSpec((ple, d), const),
            pl.BlockSpec((1, d), const),
        ],
        out_specs=(pl.BlockSpec((TOKEN_TILE, d), p_tok), pl.BlockSpec((TOKEN_TILE, d), s_tok)),
        out_shape=(jax.ShapeDtypeStruct((n_p, d), F32), jax.ShapeDtypeStruct((n_s, d), F32)),
        scratch_shapes=[pltpu.VMEM((2, TOP_K, TOKEN_TILE) + ys.shape[1:], F32), pltpu.SemaphoreType.DMA((2,))],
        compiler_params=pltpu.CompilerParams(
            dimension_semantics=("arbitrary",), vmem_limit_bytes=VMEM_LIMIT),
        name="combine",
    )(pos_flat, pos_flat, ys, topw, h1_p, h1_s, p_p, p_s, nple, wg, wp, nfin)


def kernel(x_prompt, x_sample, cache_pool, p_prompt, p_sample, norm_mix, w_in, ln_v_g, ln_v_b, w_spatial, b_spatial, w_pool_group, pool_scale, w_branch_a, w_branch_b, w_out, norm_moe, w_router, b_router, w_up, b_up, w_down, b_down, norm_ple, w_ple_gate, w_ple_proj, norm_final):
    assert norm_mix.shape[0] == 1, "single-layer stack"
    nb, seq, d = x_prompt.shape
    sb, sseq, _ = x_sample.shape
    n_prompt = nb * seq
    n_sample = sb * sseq
    n_tok = n_prompt + n_sample
    past_len = 4096
    row = lambda v: v.reshape(1, -1)

    wr = jnp.transpose(w_router[0]).astype(BF16)
    br = b_router[0].reshape(N_EXPERTS, 1)
    weights_a = (row(norm_mix[0]), w_in[0].astype(BF16), row(ln_v_g[0]), row(ln_v_b[0]), w_spatial[0],
                 jnp.transpose(b_spatial[0]), w_pool_group[0].astype(BF16), row(pool_scale[0]),
                 w_branch_a[0].astype(BF16), w_branch_b[0].astype(BF16), w_out[0].astype(BF16),
                 row(norm_moe[0]), wr, br)
    h1_p, xn2_p, route_p, rw_p, cnt_p, pool_p = _stage_a_prompt(x_prompt, weights_a)
    h1_s, xn2_s, route_s, rw_s, cnt_s, pool_s, vn_s = _stage_a_sample(x_sample, cache_pool[0], past_len, weights_a)

    n_tab = jnp.concatenate([cnt_p[:, :, 0], cnt_s[:, :, 0]], axis=0).astype(I32)
    ahead = jnp.cumsum(n_tab, axis=0) - n_tab
    total = jnp.sum(n_tab, axis=0)
    tiles_e = (total + (RUN_WINDOW - 1) + (ROW_TILE - 1)) // ROW_TILE
    ends = jnp.cumsum(tiles_e)
    first = (ends - tiles_e) * ROW_TILE
    num_tiles = ends[-1:]
    max_tiles = (n_tok * TOP_K + N_EXPERTS * (RUN_WINDOW - 1)) // ROW_TILE + N_EXPERTS
    tile_ids = jnp.minimum(jnp.arange(max_tiles, dtype=I32), num_tiles[0] - 1)
    tile_expert = jnp.sum((tile_ids[:, None] >= ends[None, :]).astype(I32), axis=1)
    run_dst = (first[None, :] + ahead).reshape(-1)
    run_win = (n_tab + (RUN_WINDOW - 1)) // RUN_WINDOW
    run_src = ((jnp.cumsum(run_win, axis=1) - run_win) * RUN_WINDOW).reshape(-1)
    pad_lo = first + total
    pad_hi = ends * ROW_TILE

    route = jnp.concatenate([route_p, route_s], axis=0)
    topw = jnp.concatenate([rw_p, rw_s], axis=0)
    lpos_flat = route[:, :TOP_K].reshape(-1)

    xs, pos = _dispatch(lpos_flat, route, run_dst, run_src, run_win.reshape(-1), pad_lo, pad_hi, num_tiles,
                        xn2_p, xn2_s, max_tiles * ROW_TILE)
    ys = _experts(tile_expert, num_tiles, xs, w_up[0], b_up[0], w_down[0], b_down[0])

    y_p, y_s = _combine(pos.reshape(-1), ys, topw, h1_p, h1_s, p_prompt[0].reshape(n_prompt, -1),
                        p_sample[0].reshape(n_sample, -1), row(norm_ple[0]), w_ple_gate[0].astype(BF16),
                        w_ple_proj[0].astype(BF16), row(norm_final))

    return (y_p.reshape(nb, seq, d), y_s.reshape(sb, sseq, d), pool_p[None], pool_s[None],
            vn_s.reshape(1, sb, sseq, d))
```

```python
import functools

import jax
import jax.numpy as jnp
from jax import lax
from jax.experimental import pallas as pl
from jax.experimental.pallas import tpu as pltpu

F32 = jnp.float32
BF16 = jnp.bfloat16
I32 = jnp.int32

EPS = 1e-6
CHUNK = 64
GMLP_BLOCK = 128
GMLP_HEADS = 4
POOL_WINDOWS = (2, 4, 8, 16)
POOL_STATE = 15
SUBLANES = 8
HALO = SUBLANES * len(POOL_WINDOWS)
N_EXPERTS = 32
TOP_K = 4
SWIGLU_LIMIT = 7.0
SWIGLU_ALPHA = 1.702
LANES = 128

STAGE_TILE = 512
TOKEN_TILE = 512
ROW_TILE = 752
RUN_WINDOW = 16
LOCAL_ROWS = N_EXPERTS * RUN_WINDOW + TOKEN_TILE * TOP_K
VMEM_LIMIT = 56 * 1024 * 1024


def _rms(xf, g):
    return xf * lax.rsqrt(jnp.mean(xf * xf, axis=-1, keepdims=True) + EPS) * g


def _dot(a, b):
    return jnp.dot(a, b, preferred_element_type=F32)


ROUTE_ROWS = 4 * TOP_K
WEIGHT_ROWS = 2 * TOP_K


def _route_tile(logits):
    ne, nt = logits.shape
    eid = lax.broadcasted_iota(I32, (ne, nt), 0).astype(F32)
    neg = jnp.float32(-jnp.inf)
    lg = logits
    picked = jnp.zeros((ne, nt), F32)
    vals, sels, idxs = [], [], []
    for _ in range(TOP_K):
        m = jnp.max(lg, axis=0, keepdims=True)
        idx = jnp.min(jnp.where(lg == m, eid, float(ne)), axis=0, keepdims=True)
        sel = eid == idx
        lg = jnp.where(sel, neg, lg)
        picked = picked + sel.astype(F32)
        vals.append(m)
        sels.append(sel)
        idxs.append(idx)
    ex = [jnp.exp(v - vals[0]) for v in vals]
    den = ex[0] + ex[1] + ex[2] + ex[3]

    earlier = (lax.broadcasted_iota(I32, (nt, nt), 0) < lax.broadcasted_iota(I32, (nt, nt), 1)).astype(BF16)
    before = _dot(picked.astype(BF16), earlier)
    counts = jnp.sum(picked, axis=1, keepdims=True)
    windows = jnp.floor((counts + (RUN_WINDOW - 1)) * (1.0 / RUN_WINDOW))
    lower = (lax.broadcasted_iota(I32, (ne, ne), 0) > lax.broadcasted_iota(I32, (ne, ne), 1)).astype(BF16)
    run_start = _dot(lower, jnp.broadcast_to(windows, (ne, LANES)).astype(BF16))[:, 0:1] * float(RUN_WINDOW)

    zero_row = jnp.zeros((1, nt), F32)
    ranks = [jnp.sum(jnp.where(s, before, 0.0), axis=0, keepdims=True) for s in sels]
    places = [jnp.sum(jnp.where(s, run_start + before, 0.0), axis=0, keepdims=True) for s in sels]
    packed = jnp.concatenate(places + idxs + ranks + [zero_row] * (ROUTE_ROWS - 3 * TOP_K), axis=0)
    wts = jnp.concatenate([e / den for e in ex] + [zero_row] * (WEIGHT_ROWS - TOP_K), axis=0)
    return packed.astype(I32), wts, counts


def _stage_a_kernel(n_seq, seq_rows, carry_halo, pos0, *refs):
    if carry_halo:
        (x_ref, nm_ref, win_ref, lng_ref, lnb_ref, wsp_ref, bsp_ref, wpool_ref, pscale_ref,
         wa_ref, wb_ref, wout_ref, nmoe_ref, wr_ref, br_ref,
         h1_ref, xn2_ref, route_ref, rw_ref, cnt_ref, pool_ref, vn_ref, xc_ref, lv_ref, a_ref) = refs
        cache_ref = None
    else:
        (x_ref, cache_ref, nm_ref, win_ref, lng_ref, lnb_ref, wsp_ref, bsp_ref, wpool_ref,
         pscale_ref, wa_ref, wb_ref, wout_ref, nmoe_ref, wr_ref, br_ref,
         h1_ref, xn2_ref, route_ref, rw_ref, cnt_ref, pool_ref, vn_ref, xc_ref, lv_ref, a_ref) = refs
    d = x_ref.shape[-1]
    head_dim = d // GMLP_HEADS
    group_dim = d // len(POOL_WINDOWS)
    blk = min(GMLP_BLOCK, seq_rows)
    j = pl.program_id(1) if carry_halo else 0

    x = x_ref[...]
    n = _rms(x, nm_ref[...]).astype(BF16)

    def proj(seg):
        return _dot(n, win_ref[:, seg * d:(seg + 1) * d])

    v = jax.nn.gelu(proj(1))
    mu = jnp.mean(v, axis=-1, keepdims=True)
    vc = v - mu
    vn_ref[...] = vc * lax.rsqrt(jnp.mean(vc * vc, axis=-1, keepdims=True) + EPS) * lng_ref[...] + lnb_ref[...]
    u = jax.nn.gelu(proj(0))

    ri = lax.broadcasted_iota(I32, (GMLP_BLOCK, GMLP_BLOCK), 0) // CHUNK
    ci = lax.broadcasted_iota(I32, (GMLP_BLOCK, GMLP_BLOCK), 1) // CHUNK
    causal = (ri >= ci).astype(F32)
    for h in range(GMLP_HEADS):
        wm = (wsp_ref[h] * causal)[:blk, :blk].astype(BF16)
        bias = bsp_ref[:blk, h:h + 1]
        cols = slice(h * head_dim, (h + 1) * head_dim)
        for r0 in range(0, n_seq * seq_rows, blk):
            rows = slice(r0, r0 + blk)
            sg = _dot(wm, vn_ref[rows, cols].astype(BF16)) + bias
            a_ref[rows, cols] = (u[rows, cols] * sg).astype(BF16)
    y_a = _dot(a_ref[...], wa_ref[...])

    pb = proj(2)
    if carry_halo:
        @pl.when(j == 0)
        def _():
            xc_ref[0, 0:HALO, :] = jnp.zeros((HALO, d), F32)
    else:
        for s in range(n_seq):
            xc_ref[s, 0:HALO, :] = jnp.zeros((HALO, d), F32)
            xc_ref[s, HALO - POOL_STATE:HALO, :] = cache_ref[s]
    for s in range(n_seq):
        xc_ref[s, HALO:HALO + seq_rows, :] = pb[s * seq_rows:(s + 1) * seq_rows]

    end = HALO + seq_rows
    window_sums = [[] for _ in POOL_WINDOWS]
    for s in range(n_seq):
        src, shift = xc_ref, 1
        for lvl in range(len(POOL_WINDOWS)):
            lo = SUBLANES * (lvl + 1)
            cols = slice(lvl * group_dim, d)
            summed = src[s, lo:end, cols] + src[s, lo - shift:end - shift, cols]
            window_sums[lvl].append(summed[HALO - lo:, :group_dim])
            if lvl + 1 < len(POOL_WINDOWS):
                dst = lv_ref.at[lvl % 2]
                dst[s, lo:end, (lvl + 1) * group_dim:d] = summed[:, group_dim:]
                src, shift = dst, 2 * shift

    if carry_halo:
        pos = pos0 + j * seq_rows + lax.broadcasted_iota(I32, (seq_rows, 1), 0)
    z_groups = []
    for g, w in enumerate(POOL_WINDOWS):
        cols = slice(g * group_dim, (g + 1) * group_dim)
        if carry_halo:
            cnt = jnp.minimum(pos + 1, w).astype(F32)
        else:
            cnt = float(min(pos0 + 1, w))
        parts = [window_sums[g][s] / cnt - xc_ref[s, HALO:end, cols] for s in range(n_seq)]
        zg = parts[0] if n_seq == 1 else jnp.concatenate(parts, axis=0)
        z_groups.append(_dot(zg.astype(BF16), wpool_ref[g]) * pscale_ref[:, cols])
    z = jnp.concatenate(z_groups, axis=1)
    y_b = _dot(z.astype(BF16), wb_ref[...])

    for s in range(n_seq):
        tail = xc_ref[s, HALO + seq_rows - POOL_STATE:HALO + seq_rows, :]
        if carry_halo:
            @pl.when(j == pl.num_programs(1) - 1)
            def _():
                pool_ref[...] = tail
        else:
            pool_ref[s] = tail
    if carry_halo:
        xc_ref[0, 0:HALO, :] = xc_ref[0, seq_rows:seq_rows + HALO, :]

    g_a = jax.nn.sigmoid(proj(3))
    g_b = jax.nn.sigmoid(proj(4))
    merged = (g_a * y_a + g_b * y_b).astype(BF16)
    h1 = x + _dot(merged, wout_ref[...])
    h1_ref[...] = h1

    xn2 = _rms(h1, nmoe_ref[...]).astype(BF16)
    xn2_ref[...] = xn2.astype(F32).reshape(xn2_ref.shape)
    for t in range(n_seq * seq_rows // TOKEN_TILE):
        rows = slice(t * TOKEN_TILE, (t + 1) * TOKEN_TILE)
        logits = lax.dot_general(wr_ref[...], xn2[rows], (((1,), (1,)), ((), ())),
                                 preferred_element_type=F32) + br_ref[...]
        packed, wts, counts = _route_tile(logits)
        route_ref[t] = packed
        rw_ref[t] = wts
        cnt_ref[t] = jnp.broadcast_to(counts, (N_EXPERTS, LANES))


def _const_spec(shape, grid_rank):
    zeros = (0,) * len(shape)
    if grid_rank == 2:
        return pl.BlockSpec(shape, lambda b, j: zeros, pipeline_mode=pl.Buffered(1))
    return pl.BlockSpec(shape, lambda i: zeros, pipeline_mode=pl.Buffered(1))


def _stage_a_outputs(n_rows, d):
    return (jax.ShapeDtypeStruct((n_rows, d), F32),
            jax.ShapeDtypeStruct((n_rows, SUBLANES, d // SUBLANES), F32),
            jax.ShapeDtypeStruct((n_rows // TOKEN_TILE, ROUTE_ROWS, TOKEN_TILE), I32),
            jax.ShapeDtypeStruct((n_rows // TOKEN_TILE, WEIGHT_ROWS, TOKEN_TILE), F32),
            jax.ShapeDtypeStruct((n_rows // TOKEN_TILE, N_EXPERTS, LANES), F32))


def _stage_a_prompt(x_prompt, weights):
    nb, seq, d = x_prompt.shape
    tiles_per_seq = seq // STAGE_TILE
    sub = STAGE_TILE // TOKEN_TILE
    tok_map = lambda b, j: (b * tiles_per_seq + j, 0)
    tok_map3 = lambda b, j: (b * tiles_per_seq + j, 0, 0)
    return pl.pallas_call(
        functools.partial(_stage_a_kernel, 1, STAGE_TILE, True, 0),
        grid=(nb, tiles_per_seq),
        in_specs=[pl.BlockSpec((None, STAGE_TILE, d), lambda b, j: (b, j, 0))]
        + [_const_spec(w.shape, 2) for w in weights],
        out_specs=(
            pl.BlockSpec((STAGE_TILE, d), tok_map),
            pl.BlockSpec((STAGE_TILE, SUBLANES, d // SUBLANES), tok_map3),
            pl.BlockSpec((sub, ROUTE_ROWS, TOKEN_TILE), tok_map3),
            pl.BlockSpec((sub, WEIGHT_ROWS, TOKEN_TILE), tok_map3),
            pl.BlockSpec((sub, N_EXPERTS, LANES), tok_map3),
            pl.BlockSpec((None, POOL_STATE, d), lambda b, j: (b, 0, 0)),
        ),
        out_shape=_stage_a_outputs(nb * seq, d) + (jax.ShapeDtypeStruct((nb, POOL_STATE, d), F32),),
        scratch_shapes=[
            pltpu.VMEM((STAGE_TILE, d), F32),
            pltpu.VMEM((1, HALO + STAGE_TILE, d), F32),
            pltpu.VMEM((2, 1, HALO + STAGE_TILE, d), F32),
            pltpu.VMEM((STAGE_TILE, d), BF16),
        ],
        compiler_params=pltpu.CompilerParams(
            dimension_semantics=("arbitrary", "arbitrary"), vmem_limit_bytes=VMEM_LIMIT),
        name="stage_a_prompt",
    )(x_prompt, *weights)


def _stage_a_sample(x_sample, cache, past_len, weights):
    sb, sseq, d = x_sample.shape
    n_s = sb * sseq
    whole2 = lambda i: (0, 0)
    whole3 = lambda i: (0, 0, 0)
    return pl.pallas_call(
        functools.partial(_stage_a_kernel, sb, sseq, False, past_len),
        grid=(1,),
        in_specs=[pl.BlockSpec((n_s, d), whole2), pl.BlockSpec((sb, POOL_STATE, d), whole3)]
        + [_const_spec(w.shape, 1) for w in weights],
        out_specs=(
            pl.BlockSpec((n_s, d), whole2),
            pl.BlockSpec((n_s, SUBLANES, d // SUBLANES), whole3),
            pl.BlockSpec((n_s // TOKEN_TILE, ROUTE_ROWS, TOKEN_TILE), whole3),
            pl.BlockSpec((n_s // TOKEN_TILE, WEIGHT_ROWS, TOKEN_TILE), whole3),
            pl.BlockSpec((n_s // TOKEN_TILE, N_EXPERTS, LANES), whole3),
            pl.BlockSpec((sb, POOL_STATE, d), whole3),
            pl.BlockSpec((n_s, d), whole2),
        ),
        out_shape=_stage_a_outputs(n_s, d) + (jax.ShapeDtypeStruct((sb, POOL_STATE, d), F32),
                                              jax.ShapeDtypeStruct((n_s, d), F32)),
        scratch_shapes=[
            pltpu.VMEM((sb, HALO + sseq, d), F32),
            pltpu.VMEM((2, sb, HALO + sseq, d), F32),
            pltpu.VMEM((n_s, d), BF16),
        ],
        compiler_params=pltpu.CompilerParams(
            dimension_semantics=("arbitrary",), vmem_limit_bytes=VMEM_LIMIT),
        name="stage_a_sample",
    )(x_sample.reshape(n_s, d), cache, *weights)


def _for_tile_rows(fn):
    def body(g, c):
        for rr in range(SUBLANES):
            for k in range(TOP_K):
                fn(g * SUBLANES + rr, k)
        return c

    lax.fori_loop(0, TOKEN_TILE // SUBLANES, body, 0)


def _loop(lo, hi, fn):
    def body(i, c):
        fn(i)
        return c

    lax.fori_loop(lo, hi, body, 0)


def _dispatch_kernel(n_prompt_tiles, lpos_ref, run_dst_ref, run_src_ref, run_win_ref, pad_lo_ref, pad_hi_ref,
                     nt_ref, route_ref, xp_ref, xsm_ref, xs_hbm, pos_ref, lbuf, zbuf, sems):
    i = pl.program_id(0)
    last = pl.num_programs(0) - 1
    slot = i % 2

    routed = route_ref[0]
    expert = routed[TOP_K:2 * TOP_K]
    group_row = jnp.zeros(expert.shape, I32)
    for e in range(N_EXPERTS):
        group_row = jnp.where(expert == e, run_dst_ref[i * N_EXPERTS + e], group_row)
    pos_ref[0] = group_row + routed[2 * TOP_K:3 * TOP_K]

    @pl.when(i == 0)
    def _():
        lbuf[...] = jnp.zeros(lbuf.shape, F32)
        zbuf[...] = jnp.zeros(zbuf.shape, F32)
        pad_sem = sems.at[0, 0]

        def pad_group(e):
            lo, hi = pad_lo_ref[e], pad_hi_ref[e]
            n_big = (hi - lo) // RUN_WINDOW
            mid = lo + n_big * RUN_WINDOW
            n_mid = (hi - mid) // SUBLANES
            fine = mid + n_mid * SUBLANES
            big = lambda c: pltpu.make_async_copy(zbuf.at[pl.ds(0, RUN_WINDOW)],
                                                  xs_hbm.at[pl.ds(lo + c * RUN_WINDOW, RUN_WINDOW)], pad_sem)
            med = lambda c: pltpu.make_async_copy(zbuf.at[pl.ds(0, SUBLANES)],
                                                  xs_hbm.at[pl.ds(mid + c * SUBLANES, SUBLANES)], pad_sem)
            one = lambda r: pltpu.make_async_copy(zbuf.at[0], xs_hbm.at[r], pad_sem)
            _loop(0, n_big, lambda c: big(c).start())
            _loop(0, n_mid, lambda c: med(c).start())
            _loop(fine, hi, lambda r: one(r).start())
            _loop(0, n_big, lambda c: big(c).wait())
            _loop(0, n_mid, lambda c: med(c).wait())
            _loop(fine, hi, lambda r: one(r).wait())

        _loop(0, N_EXPERTS, pad_group)

        whole = lambda t: pltpu.make_async_copy(zbuf, xs_hbm.at[pl.ds(t * ROW_TILE, ROW_TILE)], pad_sem)
        _loop(nt_ref[0], xs_hbm.shape[0] // ROW_TILE, lambda t: whole(t).start())
        _loop(nt_ref[0], xs_hbm.shape[0] // ROW_TILE, lambda t: whole(t).wait())

    def place(src_ref):
        def body(r, k):
            lbuf[slot, lpos_ref[k * TOKEN_TILE + r]] = src_ref[r]
        _for_tile_rows(body)

    @pl.when(i < n_prompt_tiles)
    def _():
        place(xp_ref)

    @pl.when(i >= n_prompt_tiles)
    def _():
        place(xsm_ref)

    def run_copies(tile, s, e, fn):
        t = tile * N_EXPERTS + e
        src0, dst0 = run_src_ref[t], run_dst_ref[t]
        _loop(0, run_win_ref[t], lambda w: fn(pltpu.make_async_copy(
            lbuf.at[s, pl.ds(src0 + w * RUN_WINDOW, RUN_WINDOW)],
            xs_hbm.at[pl.ds(dst0 + w * RUN_WINDOW, RUN_WINDOW)], sems.at[s, e])))

    def per_pair(e2):
        for queue in range(2):
            e = e2 * 2 + queue

            @pl.when(i > 0)
            def _():
                run_copies(i - 1, 1 - slot, e, lambda c: c.wait())

            run_copies(i, slot, e, lambda c: c.start(priority=queue))

    _loop(0, N_EXPERTS // 2, per_pair)

    @pl.when(i == last)
    def _():
        _loop(0, N_EXPERTS, lambda e: run_copies(i, slot, e, lambda c: c.wait()))


def _dispatch(lpos_flat, route, run_dst, run_src, run_win, pad_lo, pad_hi, num_tiles, xn2_p, xn2_s, n_rows):
    row_shape = xn2_p.shape[1:]
    npt = xn2_p.shape[0] // TOKEN_TILE
    nst = xn2_s.shape[0] // TOKEN_TILE
    smem_all = lambda a: pl.BlockSpec(a.shape, lambda i: (0,), memory_space=pltpu.SMEM)
    return pl.pallas_call(
        functools.partial(_dispatch_kernel, npt),
        grid=(npt + nst,),
        in_specs=[pl.BlockSpec((TOKEN_TILE * TOP_K,), lambda i: (i,), memory_space=pltpu.SMEM),
                  smem_all(run_dst), smem_all(run_src), smem_all(run_win),
                  smem_all(pad_lo), smem_all(pad_hi), smem_all(num_tiles),
                  pl.BlockSpec((1, ROUTE_ROWS, TOKEN_TILE), lambda i: (i, 0, 0)),
                  pl.BlockSpec((TOKEN_TILE,) + row_shape, lambda i: (jnp.minimum(i, npt - 1), 0, 0)),
                  pl.BlockSpec((TOKEN_TILE,) + row_shape, lambda i: (jnp.maximum(i - npt, 0), 0, 0))],
        out_specs=(pl.BlockSpec(memory_space=pl.ANY),
                   pl.BlockSpec((1, TOP_K, TOKEN_TILE), lambda i: (i, 0, 0))),
        out_shape=(jax.ShapeDtypeStruct((n_rows,) + row_shape, F32),
                   jax.ShapeDtypeStruct((npt + nst, TOP_K, TOKEN_TILE), I32)),
        scratch_shapes=[pltpu.VMEM((2, LOCAL_ROWS) + row_shape, F32),
                        pltpu.VMEM((ROW_TILE,) + row_shape, F32),
                        pltpu.SemaphoreType.DMA((2, N_EXPERTS))],
        compiler_params=pltpu.CompilerParams(
            dimension_semantics=("arbitrary",), vmem_limit_bytes=VMEM_LIMIT),
        name="dispatch",
    )(lpos_flat, run_dst, run_src, run_win, pad_lo, pad_hi, num_tiles, route, xn2_p, xn2_s)


def _expert_kernel(te_ref, nt_ref, xs_ref, wup_ref, bup_ref, wdn_ref, bdn_ref, ys_ref):
    d_ff = wdn_ref.shape[0]
    used = pl.program_id(0) < nt_ref[0]

    @pl.when(used)
    def _():
        x = xs_ref[...].reshape(ROW_TILE, -1).astype(BF16)
        hid = _dot(x, wup_ref[...].astype(BF16)) + bup_ref[...]
        glu = jnp.minimum(hid[:, :d_ff], SWIGLU_LIMIT)
        lin = jnp.clip(hid[:, d_ff:], -SWIGLU_LIMIT, SWIGLU_LIMIT)
        act = glu * jax.nn.sigmoid(SWIGLU_ALPHA * glu) * (lin + 1.0)
        y = _dot(act.astype(BF16), wdn_ref[...].astype(BF16)) + bdn_ref[...]
        ys_ref[...] = y.reshape(ys_ref.shape)

    @pl.when(jnp.logical_not(used))
    def _():
        ys_ref[...] = jnp.zeros(ys_ref.shape, F32)


def _experts(tile_expert, num_tiles, xs, w_up, b_up, w_down, b_down):
    n_rows = xs.shape[0]
    row_shape = xs.shape[1:]
    n_e, d, d_up = w_up.shape
    d_ff = w_down.shape[1]
    exp_map = lambda i, te, nt: (te[i], 0, 0)
    grid_spec = pltpu.PrefetchScalarGridSpec(
        num_scalar_prefetch=2,
        grid=(n_rows // ROW_TILE,),
        in_specs=[
            pl.BlockSpec((ROW_TILE,) + row_shape, lambda i, te, nt: (jnp.minimum(i, nt[0] - 1), 0, 0)),
            pl.BlockSpec((None, d, d_up), exp_map),
            pl.BlockSpec((None, 1, d_up), exp_map),
            pl.BlockSpec((None, d_ff, d), exp_map),
            pl.BlockSpec((None, 1, d), exp_map),
        ],
        out_specs=pl.BlockSpec((ROW_TILE,) + row_shape, lambda i, te, nt: (i, 0, 0)),
    )
    return pl.pallas_call(
        _expert_kernel,
        grid_spec=grid_spec,
        out_shape=jax.ShapeDtypeStruct((n_rows,) + row_shape, F32),
        compiler_params=pltpu.CompilerParams(
            dimension_semantics=("arbitrary",), vmem_limit_bytes=VMEM_LIMIT),
        name="experts",
    )(tile_expert, num_tiles, xs, w_up, b_up.reshape(n_e, 1, d_up), w_down, b_down.reshape(n_e, 1, d))


def _combine_kernel(n_prompt_tiles, pos_ref, pos_next_ref, ys_hbm, w_ref, h1p_ref, h1s_ref, pp_ref, ps_ref,
                    nple_ref, wg_ref, wp_ref, nfin_ref, yp_ref, ysm_ref, gbuf, sems):
    i = pl.program_id(0)
    is_prompt = i < n_prompt_tiles
    slot = i % 2

    def gather(p_ref, s):
        return lambda r, k: pltpu.make_async_copy(ys_hbm.at[p_ref[k * TOKEN_TILE + r]], gbuf.at[s, k, r],
                                                  sems.at[s])

    @pl.when(i == 0)
    def _():
        copy = gather(pos_ref, 0)
        _for_tile_rows(lambda r, k: copy(r, k).start(priority=k % 2))

    @pl.when(i + 1 < pl.num_programs(0))
    def _():
        copy = gather(pos_next_ref, 1 - slot)
        _for_tile_rows(lambda r, k: copy(r, k).start(priority=k % 2))

    copy = gather(pos_ref, slot)
    _for_tile_rows(lambda r, k: copy(r, k).wait())

    d = h1p_ref.shape[-1]
    wts = jnp.transpose(w_ref[0])
    moe = wts[:, 0:1] * gbuf[slot, 0].reshape(TOKEN_TILE, d)
    for k in range(1, TOP_K):
        moe = moe + wts[:, k:k + 1] * gbuf[slot, k].reshape(TOKEN_TILE, d)
    h2 = jnp.where(is_prompt, h1p_ref[...], h1s_ref[...]) + moe
    p = jnp.where(is_prompt, pp_ref[...], ps_ref[...])
    n3 = _rms(h2, nple_ref[...]).astype(BF16)
    gate = jax.nn.sigmoid(_dot(n3, wg_ref[...]))
    h3 = h2 + gate * _dot(p.astype(BF16), wp_ref[...])
    y = _rms(h3, nfin_ref[...])

    @pl.when(is_prompt)
    def _():
        yp_ref[...] = y

    @pl.when(jnp.logical_not(is_prompt))
    def _():
        ysm_ref[...] = y


def _combine(pos_flat, ys, topw, h1_p, h1_s, p_p, p_s, nple, wg, wp, nfin):
    n_p, d = h1_p.shape
    n_s = h1_s.shape[0]
    ple = p_p.shape[1]
    npt = n_p // TOKEN_TILE
    nst = n_s // TOKEN_TILE
    p_tok = lambda i: (jnp.minimum(i, npt - 1), 0)
    s_tok = lambda i: (jnp.maximum(i - npt, 0), 0)
    const = lambda i: (0, 0)
    last = npt + nst - 1
    this_tile = pl.BlockSpec((TOKEN_TILE * TOP_K,), lambda i: (i,), memory_space=pltpu.SMEM)
    next_tile = pl.BlockSpec((TOKEN_TILE * TOP_K,), lambda i: (jnp.minimum(i + 1, last),), memory_space=pltpu.SMEM)
    return pl.pallas_call(
        functools.partial(_combine_kernel, npt),
        grid=(npt + nst,),
        in_specs=[
            this_tile, next_tile,
            pl.BlockSpec(memory_space=pl.ANY),
            pl.BlockSpec((1, WEIGHT_ROWS, TOKEN_TILE), lambda i: (i, 0, 0)),
            pl.BlockSpec((TOKEN_TILE, d), p_tok),
            pl.BlockSpec((TOKEN_TILE, d), s_tok),
            pl.BlockSpec((TOKEN_TILE, ple), p_tok),
            pl.BlockSpec((TOKEN_TILE, ple), s_tok),
            pl.BlockSpec((1, d), const),
            pl.BlockSpec((d, d), const),
            pl.BlockSpec((ple, d), const),
            pl.BlockSpec((1, d), const),
        ],
        out_specs=(pl.BlockSpec((TOKEN_TILE, d), p_tok), pl.BlockSpec((TOKEN_TILE, d), s_tok)),
        out_shape=(jax.ShapeDtypeStruct((n_p, d), F32), jax.ShapeDtypeStruct((n_s, d), F32)),
        scratch_shapes=[pltpu.VMEM((2, TOP_K, TOKEN_TILE) + ys.shape[1:], F32), pltpu.SemaphoreType.DMA((2,))],
        compiler_params=pltpu.CompilerParams(
            dimension_semantics=("arbitrary",), vmem_limit_bytes=VMEM_LIMIT),
        name="combine",
    )(pos_flat, pos_flat, ys, topw, h1_p, h1_s, p_p, p_s, nple, wg, wp, nfin)


def kernel(x_prompt, x_sample, cache_pool, p_prompt, p_sample, norm_mix, w_in, ln_v_g, ln_v_b, w_spatial, b_spatial, w_pool_group, pool_scale, w_branch_a, w_branch_b, w_out, norm_moe, w_router, b_router, w_up, b_up, w_down, b_down, norm_ple, w_ple_gate, w_ple_proj, norm_final):
    assert norm_mix.shape[0] == 1, "single-layer stack"
    nb, seq, d = x_prompt.shape
    sb, sseq, _ = x_sample.shape
    n_prompt = nb * seq
    n_sample = sb * sseq
    n_tok = n_prompt + n_sample
    past_len = 4096
    row = lambda v: v.reshape(1, -1)

    wr = jnp.transpose(w_router[0]).astype(BF16)
    br = b_router[0].reshape(N_EXPERTS, 1)
    weights_a = (row(norm_mix[0]), w_in[0].astype(BF16), row(ln_v_g[0]), row(ln_v_b[0]), w_spatial[0],
                 jnp.transpose(b_spatial[0]), w_pool_group[0].astype(BF16), row(pool_scale[0]),
                 w_branch_a[0].astype(BF16), w_branch_b[0].astype(BF16), w_out[0].astype(BF16),
                 row(norm_moe[0]), wr, br)
    h1_p, xn2_p, route_p, rw_p, cnt_p, pool_p = _stage_a_prompt(x_prompt, weights_a)
    h1_s, xn2_s, route_s, rw_s, cnt_s, pool_s, vn_s = _stage_a_sample(x_sample, cache_pool[0], past_len, weights_a)

    n_tab = jnp.concatenate([cnt_p[:, :, 0], cnt_s[:, :, 0]], axis=0).astype(I32)
    ahead = jnp.cumsum(n_tab, axis=0) - n_tab
    total = jnp.sum(n_tab, axis=0)
    tiles_e = (total + (RUN_WINDOW - 1) + (ROW_TILE - 1)) // ROW_TILE
    ends = jnp.cumsum(tiles_e)
    first = (ends - tiles_e) * ROW_TILE
    num_tiles = ends[-1:]
    max_tiles = (n_tok * TOP_K + N_EXPERTS * (RUN_WINDOW - 1)) // ROW_TILE + N_EXPERTS
    tile_ids = jnp.minimum(jnp.arange(max_tiles, dtype=I32), num_tiles[0] - 1)
    tile_expert = jnp.sum((tile_ids[:, None] >= ends[None, :]).astype(I32), axis=1)
    run_dst = (first[None, :] + ahead).reshape(-1)
    run_win = (n_tab + (RUN_WINDOW - 1)) // RUN_WINDOW
    run_src = ((jnp.cumsum(run_win, axis=1) - run_win) * RUN_WINDOW).reshape(-1)
    pad_lo = first + total
    pad_hi = ends * ROW_TILE

    route = jnp.concatenate([route_p, route_s], axis=0)
    topw = jnp.concatenate([rw_p, rw_s], axis=0)
    lpos_flat = route[:, :TOP_K].reshape(-1)

    xs, pos = _dispatch(lpos_flat, route, run_dst, run_src, run_win.reshape(-1), pad_lo, pad_hi, num_tiles,
                        xn2_p, xn2_s, max_tiles * ROW_TILE)
    ys = _experts(tile_expert, num_tiles, xs, w_up[0], b_up[0], w_down[0], b_down[0])

    y_p, y_s = _combine(pos.reshape(-1), ys, topw, h1_p, h1_s, p_prompt[0].reshape(n_prompt, -1),
                        p_sample[0].reshape(n_sample, -1), row(norm_ple[0]), w_ple_gate[0].astype(BF16),
                        w_ple_proj[0].astype(BF16), row(norm_final))

    return (y_p.reshape(nb, seq, d), y_s.reshape(sb, sseq, d), pool_p[None], pool_s[None],
            vn_s.reshape(1, sb, sseq, d))
```
